```python
import jax, jax.numpy as jnp
from jax import lax
import numpy as np

D_MODEL = 2048
BATCH = 32
SEQ = 256
DEPTH = 2
DEC_BATCH = 8
DEC_SEQ = 2048
PAST_LEN = 512

GRID_W = 64
BLOCK = 128
EPS = 1e-6
A_WIDTH = 2048
A_GROUPS = 16
A_GROUP_DIM = A_WIDTH // A_GROUPS
B_WIDTH = 2048
B_HEAD_DIM = 64
B_HEADS = B_WIDTH // B_HEAD_DIM
B_GROUPS = 8
B_STATE = 128
B_CONV = 5
B_CONV_CH = B_WIDTH + 2 * B_GROUPS * B_STATE
L0_IN = 2 * A_WIDTH + A_WIDTH + B_WIDTH + B_CONV_CH + 2 * B_HEADS
L0_MIX = A_WIDTH + B_WIDTH
C_HEADS = 16
C_KV_HEADS = 4
C_HEAD_DIM = 128
C_WIDTH = C_HEADS * C_HEAD_DIM
C_KV_WIDTH = C_KV_HEADS * C_HEAD_DIM
L1_IN = C_WIDTH + 2 * C_KV_WIDTH + C_WIDTH
ROPE_THETA = 10000.0

kernel_name = "hybrid_dit_gmlp_ssd_gqa_step"


def rms_norm(x, g):
    x32 = x.astype(jnp.float32)
    y = x32 * lax.rsqrt(jnp.mean(x32 * x32, axis=-1, keepdims=True) + EPS)
    return y.astype(x.dtype) * g


def layer_norm(x, g):
    x32 = x.astype(jnp.float32)
    xc = x32 - jnp.mean(x32, axis=-1, keepdims=True)
    y = xc * lax.rsqrt(jnp.mean(xc * xc, axis=-1, keepdims=True) + EPS)
    return y.astype(x.dtype) * g


def modulation(cond, w, b):
    m = jax.nn.silu(cond) @ w + b
    if m.ndim == 2:
        m = m[:, None, :]
    return jnp.split(m, 3, axis=-1)


def chunk_mlp(hu, hv, z, v_gain, w_s, b_s):
    b, n, _ = hu.shape
    v = layer_norm(hv, v_gain).reshape(b, n // BLOCK, BLOCK, A_GROUPS, A_GROUP_DIM)
    s = jnp.einsum('gts,bcsgd->bctgd', w_s, v) + b_s.T[None, None, :, :, None]
    return hu * s.reshape(b, n, A_WIDTH) * jax.nn.silu(z)


def depthwise_conv(x, w, bias):
    y = lax.conv_general_dilated(x, w[:, None, :], window_strides=(1,),
                                 padding=[(B_CONV // 2, B_CONV // 2)],
                                 dimension_numbers=('NWC', 'WIO', 'NWC'),
                                 feature_group_count=x.shape[-1])
    return y + bias


def ssd_scan(x, dt, a, bm, cm, h0):
    b, n = x.shape[:2]
    e = B_HEADS // B_GROUPS
    nc = n // BLOCK
    xdt = (x * dt[..., None]).reshape(b, nc, BLOCK, B_GROUPS, e, B_HEAD_DIM)
    da = (dt * a).reshape(b, nc, BLOCK, B_GROUPS, e)
    bm = bm.reshape(b, nc, BLOCK, B_GROUPS, B_STATE)
    cm = cm.reshape(b, nc, BLOCK, B_GROUPS, B_STATE)
    causal = jnp.tril(jnp.ones((BLOCK, BLOCK), dtype=bool))[None, :, :, None, None]

    def step(h, inp):
        xc, dac, bc, cc = inp
        cum = jnp.cumsum(dac, axis=1)
        seg = cum[:, :, None] - cum[:, None, :]
        lmat = jnp.exp(jnp.where(causal, seg, -jnp.inf))
        cb = jnp.einsum('blgn,bsgn->blsg', cc, bc)
        y = jnp.einsum('blsge,bsgep->blgep', cb[..., None] * lmat, xc)
        y = y + jnp.einsum('blgn,bgepn->blgep', cc, h) * jnp.exp(cum)[..., None]
        to_end = jnp.exp(cum[:, -1:] - cum)[..., None]
        h = h * jnp.exp(cum[:, -1])[..., None, None] + jnp.einsum('bsgn,bsgep->bgepn', bc, xc * to_end)
        return h, y

    inputs = tuple(jnp.moveaxis(t, 1, 0) for t in (xdt, da, bm, cm))
    h_fin, y = lax.scan(step, h0.reshape(b, B_GROUPS, e, B_HEAD_DIM, B_STATE), inputs)
    y = jnp.moveaxis(y, 0, 1).reshape(b, n, B_HEADS, B_HEAD_DIM)
    return y, h_fin.reshape(b, B_HEADS, B_HEAD_DIM, B_STATE)


def ssd_mixer(xbc, z, dt_raw, conv_w, conv_b, dt_bias, a_log, d_skip, norm_g, h0_f, h0_b):
    xbc = jax.nn.silu(depthwise_conv(xbc, conv_w, conv_b))
    b, n, _ = xbc.shape
    gn = B_GROUPS * B_STATE
    xs = xbc[..., :B_WIDTH].reshape(b, n, B_HEADS, B_HEAD_DIM)
    bm = xbc[..., B_WIDTH:B_WIDTH + gn].reshape(b, n, B_GROUPS, B_STATE)
    cm = xbc[..., B_WIDTH + gn:].reshape(b, n, B_GROUPS, B_STATE)
    dt = jax.nn.softplus(dt_raw.reshape(b, n, 2, B_HEADS) + dt_bias)
    a = -jnp.exp(a_log)
    flip = lambda t: jnp.flip(t, axis=1)
    y_f, h_f = ssd_scan(xs, dt[:, :, 0], a[0], bm, cm, h0_f)
    y_b, h_b = ssd_scan(flip(xs), flip(dt[:, :, 1]), a[1], flip(bm), flip(cm), h0_b)
    y = y_f + flip(y_b) + (d_skip[0] + d_skip[1])[:, None] * xs
    y = y.reshape(b, n, B_WIDTH) * jax.nn.silu(z)
    y = rms_norm(y.reshape(b, n, B_GROUPS, B_WIDTH // B_GROUPS),
                 norm_g.reshape(B_GROUPS, B_WIDTH // B_GROUPS)).reshape(b, n, B_WIDTH)
    return y, h_f, h_b


def layer0_mixer(h, h0_f, h0_b, w_in, v_gain, w_s, b_s, conv_w, conv_b,
                 dt_bias, a_log, d_skip, ssm_norm, w_out):
    p = h @ w_in
    o1 = 2 * A_WIDTH
    o2 = o1 + A_WIDTH
    o3 = o2 + B_WIDTH
    o4 = o3 + B_CONV_CH
    uv = jax.nn.gelu(p[..., :o1])
    a_out = chunk_mlp(uv[..., :A_WIDTH], uv[..., A_WIDTH:], p[..., o1:o2], v_gain, w_s, b_s)
    b_out, h_f, h_b = ssd_mixer(p[..., o3:o4], p[..., o2:o3], p[..., o4:], conv_w, conv_b,
                                dt_bias, a_log, d_skip, ssm_norm, h0_f, h0_b)
    out = jnp.concatenate([a_out, b_out], axis=-1) @ w_out
    return out, h_f, h_b


def rope_2d(x, n):
    rows = n // GRID_W
    t_row = jnp.repeat(jnp.arange(rows), GRID_W)
    t_col = jnp.tile(jnp.arange(GRID_W), rows)
    half = C_HEAD_DIM // 2
    inv = ROPE_THETA ** (-jnp.arange(0, half, 2, dtype=jnp.float32) / half)

    def rot(xh, pos):
        ang = pos.astype(jnp.float32)[:, None] * inv[None, :]
        cos = jnp.cos(ang)[None, :, None, :].astype(xh.dtype)
        sin = jnp.sin(ang)[None, :, None, :].astype(xh.dtype)
        x1, x2 = xh[..., :half // 2], xh[..., half // 2:]
        return jnp.concatenate([x1 * cos - x2 * sin, x1 * sin + x2 * cos], axis=-1)

    return jnp.concatenate([rot(x[..., :half], t_row), rot(x[..., half:], t_col)], axis=-1)


def blocked_attention(q, k, v):
    b, n = q.shape[:2]
    g = C_HEADS // C_KV_HEADS
    qb = q.reshape(b, n // BLOCK, BLOCK, C_KV_HEADS, g, C_HEAD_DIM).swapaxes(0, 1)
    scale = C_HEAD_DIM ** -0.5

    def one_block(qblk):
        s = jnp.einsum('bqkgd,bskd->bkgqs', qblk, k).astype(jnp.float32) * scale
        p = jax.nn.softmax(s, axis=-1).astype(v.dtype)
        return jnp.einsum('bkgqs,bskd->bqkgd', p, v)

    o = lax.map(one_block, qb)
    return o.swapaxes(0, 1).reshape(b, n, C_WIDTH)


def attn_project(h, w_in, q_norm, k_norm):
    b, n, _ = h.shape
    p = h @ w_in
    q = p[..., :C_WIDTH].reshape(b, n, C_HEADS, C_HEAD_DIM)
    k = p[..., C_WIDTH:C_WIDTH + C_KV_WIDTH].reshape(b, n, C_KV_HEADS, C_HEAD_DIM)
    v = p[..., C_WIDTH + C_KV_WIDTH:C_WIDTH + 2 * C_KV_WIDTH].reshape(b, n, C_KV_HEADS, C_HEAD_DIM)
    z = p[..., C_WIDTH + 2 * C_KV_WIDTH:]
    return rms_norm(q, q_norm), rms_norm(k, k_norm), v, z


def attn_context(h, w_in, q_norm, k_norm, w_out):
    q, k, v, z = attn_project(h, w_in, q_norm, k_norm)
    o = blocked_attention(q, k, v)
    return (o * jax.nn.silu(z)) @ w_out, k, v


def attn_latent(h, ctx_k, ctx_v, w_in, q_norm, k_norm, w_out):
    n = h.shape[1]
    q, k, v, z = attn_project(h, w_in, q_norm, k_norm)
    q, k = rope_2d(q, n), rope_2d(k, n)
    o = blocked_attention(q, jnp.concatenate([ctx_k, k], axis=1), jnp.concatenate([ctx_v, v], axis=1))
    return (o * jax.nn.silu(z)) @ w_out


def setup_inputs(seed: int = 0) -> dict:
    key = jax.random.key(seed)
    ks = iter(jax.random.split(key, 64))
    d = D_MODEL

    def nrm(shape, s):
        return jax.random.normal(next(ks), shape, jnp.float32) * s

    def uni(shape, lo, hi):
        return jax.random.uniform(next(ks), shape, jnp.float32, lo, hi)

    def dt_bias():
        dt = jnp.exp(uni((2, B_HEADS), float(np.log(1e-3)), float(np.log(1e-1))))
        return dt + jnp.log(-jnp.expm1(-dt))

    inp = {}
    inp['x_prompt'] = nrm((BATCH, SEQ, d), 1.0)
    inp['x_sample'] = nrm((DEC_BATCH, DEC_SEQ, d), 1.0)
    inp['state_l0_ssm_fwd'] = nrm((DEC_BATCH, B_HEADS, B_HEAD_DIM, B_STATE), 0.1)
    inp['state_l0_ssm_bwd'] = nrm((DEC_BATCH, B_HEADS, B_HEAD_DIM, B_STATE), 0.1)
    inp['cache_l1_k'] = nrm((DEC_BATCH, PAST_LEN, C_KV_HEADS, C_HEAD_DIM), 1.0)
    inp['cache_l1_v'] = nrm((DEC_BATCH, PAST_LEN, C_KV_HEADS, C_HEAD_DIM), 1.0)
    inp['c'] = nrm((DEC_BATCH, d), 1.0)
    inp['c_ctx'] = nrm((d,), 1.0)
    inp['mod_w0'] = nrm((d, 3 * d), d ** -0.5)
    inp['mod_b0'] = nrm((3 * d,), 0.01)
    inp['norm_pre0'] = 1.0 + nrm((d,), 0.02)
    inp['norm_post0'] = 1.0 + nrm((d,), 0.02)
    inp['l0_w_in'] = nrm((d, L0_IN), d ** -0.5)
    inp['l0_v_gain'] = 1.0 + nrm((A_WIDTH,), 0.02)
    inp['l0_w_s'] = nrm((A_GROUPS, BLOCK, BLOCK), BLOCK ** -0.5)
    inp['l0_b_s'] = 1.0 + nrm((A_GROUPS, BLOCK), 0.01)
    inp['l0_conv_w'] = nrm((B_CONV, B_CONV_CH), B_CONV ** -0.5)
    inp['l0_conv_b'] = nrm((B_CONV_CH,), 0.01)
    inp['l0_dt_bias'] = dt_bias()
    inp['l0_a_log'] = jnp.log(uni((2, B_HEADS), 1.0, 16.0))
    inp['l0_d_skip'] = 1.0 + nrm((2, B_HEADS), 0.1)
    inp['l0_ssm_norm'] = 1.0 + nrm((B_WIDTH,), 0.02)
    inp['l0_w_out'] = nrm((L0_MIX, d), L0_MIX ** -0.5)
    inp['mod_w1'] = nrm((d, 3 * d), d ** -0.5)
    inp['mod_b1'] = nrm((3 * d,), 0.01)
    inp['norm_pre1'] = 1.0 + nrm((d,), 0.02)
    inp['norm_post1'] = 1.0 + nrm((d,), 0.02)
    inp['l1_w_in'] = nrm((d, L1_IN), d ** -0.5)
    inp['l1_q_norm'] = 1.0 + nrm((C_HEAD_DIM,), 0.02)
    inp['l1_k_norm'] = 1.0 + nrm((C_HEAD_DIM,), 0.02)
    inp['l1_w_out'] = nrm((C_WIDTH, d), C_WIDTH ** -0.5)
    return inp


def reference(x_prompt, x_sample, state_l0_ssm_fwd, state_l0_ssm_bwd, cache_l1_k, cache_l1_v,
              c, c_ctx,
              mod_w0, mod_b0, norm_pre0, norm_post0, l0_w_in, l0_v_gain, l0_w_s, l0_b_s,
              l0_conv_w, l0_conv_b, l0_dt_bias, l0_a_log, l0_d_skip, l0_ssm_norm, l0_w_out,
              mod_w1, mod_b1, norm_pre1, norm_post1, l1_w_in, l1_q_norm, l1_k_norm, l1_w_out):
    mod_w = (mod_w0, mod_w1)
    mod_b = (mod_b0, mod_b1)
    norm_pre = (norm_pre0, norm_pre1)
    norm_post = (norm_post0, norm_post1)
    l0_params = (l0_w_in, l0_v_gain, l0_w_s, l0_b_s, l0_conv_w, l0_conv_b,
                 l0_dt_bias, l0_a_log, l0_d_skip, l0_ssm_norm, l0_w_out)

    xp, xs = x_prompt, x_sample
    for layer in range(DEPTH):
        sp, scp, gp = modulation(c_ctx, mod_w[layer], mod_b[layer])
        ss, scs, gs = modulation(c, mod_w[layer], mod_b[layer])
        hp = rms_norm(xp, norm_pre[layer]) * (1 + scp) + sp
        hs = rms_norm(xs, norm_pre[layer]) * (1 + scs) + ss
        if layer % 2 == 0:
            zero = jnp.zeros((xp.shape[0], B_HEADS, B_HEAD_DIM, B_STATE), xp.dtype)
            op, new_fwd, new_bwd = layer0_mixer(hp, zero, zero, *l0_params)
            os_, _, _ = layer0_mixer(hs, state_l0_ssm_fwd, state_l0_ssm_bwd, *l0_params)
        else:
            op, new_k, new_v = attn_context(hp, l1_w_in, l1_q_norm, l1_k_norm, l1_w_out)
            os_ = attn_latent(hs, cache_l1_k, cache_l1_v, l1_w_in, l1_q_norm, l1_k_norm, l1_w_out)
        xp = xp + gp * rms_norm(op, norm_post[layer])
        xs = xs + gs * rms_norm(os_, norm_post[layer])

    return (xp, xs, new_fwd, new_bwd, new_k, new_v)
```

```python
import functools

import numpy as np
import jax
import jax.numpy as jnp
from jax import lax
from jax.experimental import pallas as pl
from jax.experimental.pallas import tpu as pltpu

F32 = jnp.float32
BF16 = jnp.bfloat16

EPS = 1e-6
LOG2E = 1.4426950408889634

D_MODEL = 2048
CHUNK = 128
GRID_W = 64
ROPE_THETA = 10000.0
A_WIDTH = 2048
A_GROUPS = 16
B_WIDTH = 2048
B_HEAD_DIM = 64
B_HEADS = 32
B_GROUPS = 8
B_STATE = 128
B_CONV = 5
B_GN = B_GROUPS * B_STATE
B_CONV_CH = B_WIDTH + 2 * B_GN
B_GROUP_CH = B_WIDTH // B_GROUPS
L0_MAIN = 2 * A_WIDTH + A_WIDTH + B_WIDTH + B_CONV_CH
C_HEADS = 16
C_KV_HEADS = 4
C_HEAD_DIM = 128
C_WIDTH = 2048
C_KV_WIDTH = 512
L1_IN = C_WIDTH + 2 * C_KV_WIDTH + C_WIDTH

MOD_ROWS = 16
CTX_ROW = 8
V7X_VMEM_LIMIT = 56 * 1024 * 1024


def _silu(x):
    return x * (1.0 / (1.0 + jnp.exp(-x)))


def _gelu_tanh(x):
    return x * (0.5 * (1.0 + jnp.tanh(0.7978845608028654 * (x + 0.044715 * (x * x * x)))))


def _softplus(x):
    return jnp.maximum(x, 0.0) + jnp.log1p(jnp.exp(-jnp.abs(x)))


def _rms_rows(x):
    return x * lax.rsqrt(jnp.mean(x * x, axis=-1, keepdims=True) + EPS)


def _params(sem, vmem=V7X_VMEM_LIMIT):
    return pltpu.CompilerParams(dimension_semantics=sem, vmem_limit_bytes=vmem)


def _mod_kernel(c_ref, w_ref, b_ref, o_ref):
    s = _silu(c_ref[...]).astype(BF16)
    o_ref[...] = jnp.dot(s, w_ref[...].astype(BF16), preferred_element_type=F32) + b_ref[...]


def _modulation(cond, w, b, tn=512):
    n = w.shape[1]
    m = pl.pallas_call(
        _mod_kernel,
        out_shape=jax.ShapeDtypeStruct((MOD_ROWS, n), F32),
        grid=(n // tn,),
        in_specs=[pl.BlockSpec((MOD_ROWS, D_MODEL), lambda j: (0, 0)),
                  pl.BlockSpec((D_MODEL, tn), lambda j: (0, j)),
                  pl.BlockSpec((1, tn), lambda j: (0, j))],
        out_specs=pl.BlockSpec((MOD_ROWS, tn), lambda j: (0, j)),
        compiler_params=_params(("arbitrary",)),
        name="modulation",
    )(cond, w, b.reshape(1, n))
    return m.reshape(MOD_ROWS, 3, D_MODEL)


def _mod_row_map(rows_per_cond, tm):
    if rows_per_cond is None:
        return lambda i: CTX_ROW
    return lambda i: (i * tm) // rows_per_cond


def _prenorm(x_ref, mod_ref, g_ref):
    y = _rms_rows(x_ref[...]) * g_ref[...]
    return (y * (1.0 + mod_ref[0, 1:2, :]) + mod_ref[0, 0:1, :]).astype(BF16)


def _inproj0_kernel(x_ref, mod_ref, g_ref, w_ref, wdt_ref, p_ref, dt_ref, h_ref, *, n_gelu, n_silu):
    j = pl.program_id(1)

    @pl.when(j == 0)
    def _():
        h = _prenorm(x_ref, mod_ref, g_ref)
        h_ref[...] = h
        dt_ref[...] = jnp.dot(h, wdt_ref[...], preferred_element_type=F32)

    acc = jnp.dot(h_ref[...], w_ref[...], preferred_element_type=F32)

    @pl.when(j < n_gelu)
    def _():
        p_ref[...] = _gelu_tanh(acc).astype(BF16)

    @pl.when((j >= n_gelu) & (j < n_gelu + n_silu))
    def _():
        p_ref[...] = _silu(acc).astype(BF16)

    @pl.when(j >= n_gelu + n_silu)
    def _():
        p_ref[...] = acc.astype(BF16)


def _inproj0(x, mod, g, w_main, w_dt, rows_per_cond, tm=1024, tn=512):
    m = x.shape[0]
    row = _mod_row_map(rows_per_cond, tm)
    kern = functools.partial(_inproj0_kernel, n_gelu=2 * A_WIDTH // tn, n_silu=(A_WIDTH + B_WIDTH) // tn)
    return pl.pallas_call(
        kern,
        out_shape=(jax.ShapeDtypeStruct((m, L0_MAIN), BF16),
                   jax.ShapeDtypeStruct((m, 128), F32)),
        grid=(m // tm, L0_MAIN // tn),
        in_specs=[pl.BlockSpec((tm, D_MODEL), lambda i, j: (i, 0)),
                  pl.BlockSpec((1, 3, D_MODEL), lambda i, j: (row(i), 0, 0)),
                  pl.BlockSpec((1, D_MODEL), lambda i, j: (0, 0)),
                  pl.BlockSpec((D_MODEL, tn), lambda i, j: (0, j)),
                  pl.BlockSpec((D_MODEL, 128), lambda i, j: (0, 0))],
        out_specs=(pl.BlockSpec((tm, tn), lambda i, j: (i, j)),
                   pl.BlockSpec((tm, 128), lambda i, j: (i, 0))),
        scratch_shapes=[pltpu.VMEM((tm, D_MODEL), BF16)],
        compiler_params=_params(("parallel", "arbitrary")),
        name="l0_inproj",
    )(x, mod, g, w_main, w_dt)


def _mixa_kernel(u_ref, v_ref, z_ref, vg_ref, ws_ref, bs_ref, o_ref, *, n_chunks):
    v = v_ref[...].astype(F32)
    vc = v - jnp.mean(v, axis=-1, keepdims=True)
    vn = vc * lax.rsqrt(jnp.mean(vc * vc, axis=-1, keepdims=True) + EPS) * vg_ref[...]
    vn = vn.astype(BF16)
    gd = A_WIDTH // A_GROUPS
    for c in range(n_chunks):
        rows = slice(c * CHUNK, (c + 1) * CHUNK)
        for g in range(A_GROUPS):
            cols = slice(g * gd, (g + 1) * gd)
            s = jnp.dot(ws_ref[g], vn[rows, cols], preferred_element_type=F32) + bs_ref[g]
            o = u_ref[rows, cols].astype(F32) * s * z_ref[rows, cols].astype(F32)
            o_ref[rows, cols] = o.astype(BF16)


def _mixer_a(p, v_gain, w_s, b_s, n_chunks=4):
    m = p.shape[0]
    tm = n_chunks * CHUNK
    return pl.pallas_call(
        functools.partial(_mixa_kernel, n_chunks=n_chunks),
        out_shape=jax.ShapeDtypeStruct((m, A_WIDTH), BF16),
        grid=(m // tm,),
        in_specs=[pl.BlockSpec((tm, A_WIDTH), lambda i: (i, 0)),
                  pl.BlockSpec((tm, A_WIDTH), lambda i: (i, 1)),
                  pl.BlockSpec((tm, A_WIDTH), lambda i: (i, 2)),
                  pl.BlockSpec((1, A_WIDTH), lambda i: (0, 0)),
                  pl.BlockSpec((A_GROUPS, CHUNK, CHUNK), lambda i: (0, 0, 0)),
                  pl.BlockSpec((A_GROUPS, CHUNK, CHUNK), lambda i: (0, 0, 0))],
        out_specs=pl.BlockSpec((tm, A_WIDTH), lambda i: (i, 0)),
        compiler_params=_params(("parallel",)),
        name="l0_mixer_a",
    )(p, p, p, v_gain, w_s, b_s)


def _split2(x):
    hi = x.astype(BF16)
    lo = (x - hi.astype(F32)).astype(BF16)
    return jnp.concatenate([hi, lo], axis=1)


def _split3(x):
    hi = x.astype(BF16)
    r = x - hi.astype(F32)
    mid = r.astype(BF16)
    lo = (r - mid.astype(F32)).astype(BF16)
    return hi, mid, lo


def _tri_matmul(tri, x):
    hi, mid, lo = _split3(x)
    out = jnp.dot(tri, lo, preferred_element_type=F32)
    out = out + jnp.dot(tri, mid, preferred_element_type=F32)
    return out + jnp.dot(tri, hi, preferred_element_type=F32)


def _expand(x, e2_ref):
    return jnp.dot(_split2(x), e2_ref[...], preferred_element_type=F32)


def _ssd_small(dtraw_ref, dtb_ref, alog_ref):
    dt = _softplus(dtraw_ref[...] + dtb_ref[...])
    da = dt * (-jnp.exp(alog_ref[...]))
    r = lax.broadcasted_iota(jnp.int32, (CHUNK, CHUNK), 0)
    c = lax.broadcasted_iota(jnp.int32, (CHUNK, CHUNK), 1)
    lower = jnp.where(c <= r, 1.0, 0.0).astype(BF16)
    upper = jnp.where(c >= r, 1.0, 0.0).astype(BF16)
    cum = jnp.where(c < B_HEADS, _tri_matmul(lower, da), _tri_matmul(upper, da))
    return dt, cum, c


def _conv_silu(prev_ref, main_ref, next_ref, cw_ref, cb_ref, has_prev, has_next):
    halo = prev_ref.shape[0]
    xcat = jnp.concatenate([prev_ref[...].astype(F32) * has_prev,
                            main_ref[...].astype(F32),
                            next_ref[...].astype(F32) * has_next], axis=0)
    total = xcat.shape[0]
    acc = None
    for k in range(B_CONV):
        shift = (B_CONV // 2 - k) % total
        xs = xcat if shift == 0 else pltpu.roll(xcat, shift, axis=0)
        term = xs[halo:halo + CHUNK] * cw_ref[k:k + 1, :]
        acc = term if acc is None else acc + term
    return _silu(acc + cb_ref[...])


HALO = 16


def _ssd_bwd_kernel(*refs, n_chunks, has_init, emit_final):
    it = iter(refs)
    prev_ref, main_ref, next_ref, dtraw_ref = next(it), next(it), next(it), next(it)
    cw_ref, cb_ref, dtb_ref, alog_ref, e2b_ref = next(it), next(it), next(it), next(it), next(it)
    h0_ref = next(it) if has_init else None
    act_ref, hin_ref = next(it), next(it)
    hfin_ref = next(it) if emit_final else None
    hb_ref = next(it)

    j = pl.program_id(1)
    cc = n_chunks - 1 - j

    @pl.when(j == 0)
    def _():
        for g in range(B_GROUPS):
            if has_init:
                hb_ref[g] = h0_ref[0, g * B_GROUP_CH:(g + 1) * B_GROUP_CH, :].T
            else:
                hb_ref[g] = jnp.zeros((B_STATE, B_GROUP_CH), F32)

    has_prev = (cc > 0).astype(F32)
    has_next = (cc < n_chunks - 1).astype(F32)
    act = _conv_silu(prev_ref, main_ref, next_ref, cw_ref, cb_ref, has_prev, has_next).astype(BF16)
    act_ref[...] = act

    dt, cum, lane = _ssd_small(dtraw_ref, dtb_ref, alog_ref)
    is_bwd = (lane >= B_HEADS) & (lane < 2 * B_HEADS)
    w_b = jnp.where(is_bwd, dt * jnp.exp(jnp.where(is_bwd, cum[0:1, :] - cum, 0.0)), 0.0)
    xw = (act[:, :B_WIDTH].astype(F32) * _expand(w_b, e2b_ref)).astype(BF16)
    tot = _expand(jnp.broadcast_to(jnp.exp(cum[0:1, :]), (HALO, CHUNK)), e2b_ref)[0:1, :]
    for g in range(B_GROUPS):
        hin_ref[0, g] = hb_ref[g].astype(BF16)
        b_g = act[:, B_WIDTH + g * B_STATE:B_WIDTH + (g + 1) * B_STATE]
        cols = slice(g * B_GROUP_CH, (g + 1) * B_GROUP_CH)
        upd = lax.dot_general(b_g, xw[:, cols], (((0,), (0,)), ((), ())), preferred_element_type=F32)
        hb_ref[g] = hb_ref[g] * tot[:, cols] + upd

    if emit_final:
        @pl.when(j == n_chunks - 1)
        def _():
            for g in range(B_GROUPS):
                hfin_ref[0, g * B_GROUP_CH:(g + 1) * B_GROUP_CH, :] = hb_ref[g].T


def _ssd_bwd(p, dt_raw, conv_w, conv_b, dt_bias, a_log, e2b, h0, n_batch, n_chunks, emit_final):
    m = p.shape[0]
    has_init = h0 is not None
    hb = CHUNK // HALO
    last_halo = m // HALO - 1
    blk = lambda b, j: b * n_chunks + (n_chunks - 1 - j)
    xcol = (L0_MAIN - B_CONV_CH) // B_CONV_CH
    in_specs = [
        pl.BlockSpec((HALO, B_CONV_CH), lambda b, j: (jnp.maximum(blk(b, j) * hb - 1, 0), xcol)),
        pl.BlockSpec((CHUNK, B_CONV_CH), lambda b, j: (blk(b, j), xcol)),
        pl.BlockSpec((HALO, B_CONV_CH), lambda b, j: (jnp.minimum((blk(b, j) + 1) * hb, last_halo), xcol)),
        pl.BlockSpec((CHUNK, 128), lambda b, j: (blk(b, j), 0)),
        pl.BlockSpec((8, B_CONV_CH), lambda b, j: (0, 0)),
        pl.BlockSpec((1, B_CONV_CH), lambda b, j: (0, 0)),
        pl.BlockSpec((1, 128), lambda b, j: (0, 0)),
        pl.BlockSpec((1, 128), lambda b, j: (0, 0)),
        pl.BlockSpec((2 * CHUNK, B_WIDTH), lambda b, j: (0, 0)),
    ]
    args = [p, p, p, dt_raw, conv_w, conv_b, dt_bias, a_log, e2b]
    if has_init:
        in_specs.append(pl.BlockSpec((1, B_WIDTH, B_STATE), lambda b, j: (b, 0, 0)))
        args.append(h0)
    out_shape = [jax.ShapeDtypeStruct((m, B_CONV_CH), BF16),
                 jax.ShapeDtypeStruct((n_batch * n_chunks, B_GROUPS, B_STATE, B_GROUP_CH), BF16)]
    out_specs = [pl.BlockSpec((CHUNK, B_CONV_CH), lambda b, j: (blk(b, j), 0)),
                 pl.BlockSpec((1, B_GROUPS, B_STATE, B_GROUP_CH), lambda b, j: (blk(b, j), 0, 0, 0))]
    if emit_final:
        out_shape.append(jax.ShapeDtypeStruct((n_batch, B_WIDTH, B_STATE), F32))
        out_specs.append(pl.BlockSpec((1, B_WIDTH, B_STATE), lambda b, j: (b, 0, 0)))
    return pl.pallas_call(
        functools.partial(_ssd_bwd_kernel, n_chunks=n_chunks, has_init=has_init, emit_final=emit_final),
        out_shape=tuple(out_shape),
        grid=(n_batch, n_chunks),
        in_specs=in_specs,
        out_specs=tuple(out_specs),
        scratch_shapes=[pltpu.VMEM((B_GROUPS, B_STATE, B_GROUP_CH), F32)],
        compiler_params=_params(("parallel", "arbitrary")),
        name="l0_ssd_bwd",
    )(*args)


def _ssd_fwd_kernel(*refs, n_chunks, has_init, emit_final):
    it = iter(refs)
    act_ref, dtraw_ref, z_ref, hin_ref = next(it), next(it), next(it), next(it)
    dtb_ref, alog_ref, e2f_ref, e2b_ref, dsk_ref, ng_ref = (next(it), next(it), next(it), next(it),
                                                           next(it), next(it))
    h0_ref = next(it) if has_init else None
    y_ref = next(it)
    hfin_ref = next(it) if emit_final else None
    hf_ref, yacc_ref = next(it), next(it)

    j = pl.program_id(1)

    @pl.when(j == 0)
    def _():
        for g in range(B_GROUPS):
            if has_init:
                hf_ref[g] = h0_ref[0, g * B_GROUP_CH:(g + 1) * B_GROUP_CH, :].T
            else:
                hf_ref[g] = jnp.zeros((B_STATE, B_GROUP_CH), F32)

    dt, cum, lane = _ssd_small(dtraw_ref, dtb_ref, alog_ref)
    is_fwd = lane < B_HEADS
    cum2 = cum * LOG2E
    ldt2 = jnp.log(dt) * LOG2E
    r2t = (ldt2 - cum2).T
    ldt2t = ldt2.T
    decay = jnp.exp(cum)
    dec_f = _expand(decay, e2f_ref)
    dec_b = _expand(decay, e2b_ref)
    to_end = jnp.exp(jnp.where(is_fwd, cum[CHUNK - 1:CHUNK, :] - cum, 0.0))
    w_f = _expand(jnp.where(is_fwd, dt * to_end, 0.0), e2f_ref)

    r = lax.broadcasted_iota(jnp.int32, (CHUNK, CHUNK), 0)
    c = lane
    dsk = dsk_ref[0:1, :] + dsk_ref[1:2, :]

    for g in range(B_GROUPS):
        cols = slice(g * B_GROUP_CH, (g + 1) * B_GROUP_CH)
        b_g = act_ref[:, B_WIDTH + g * B_STATE:B_WIDTH + (g + 1) * B_STATE]
        c_g = act_ref[:, B_WIDTH + B_GN + g * B_STATE:B_WIDTH + B_GN + (g + 1) * B_STATE]
        cb = lax.dot_general(c_g, b_g, (((1,), (1,)), ((), ())), preferred_element_type=F32)
        cb_f = jnp.where(c <= r, cb, 0.0)
        cb_b = jnp.where(c >= r, cb, 0.0)
        hf_g = hf_ref[g]
        y_g = jnp.dot(c_g, hf_g.astype(BF16), preferred_element_type=F32) * dec_f[:, cols]
        y_g = y_g + jnp.dot(c_g, hin_ref[0, g], preferred_element_type=F32) * dec_b[:, cols]
        x_g = act_ref[:, cols]
        y_g = y_g + dsk[:, cols] * x_g.astype(F32)
        yacc_ref[:, cols] = y_g
        for e in range(B_HEADS // B_GROUPS):
            h = g * (B_HEADS // B_GROUPS) + e
            hb = B_HEADS + h
            arg_f = jnp.minimum(cum2[:, h:h + 1] + r2t[h:h + 1, :], ldt2t[h:h + 1, :])
            arg_b = jnp.minimum(cum2[:, hb:hb + 1] + r2t[hb:hb + 1, :], ldt2t[hb:hb + 1, :])
            mm = (cb_f * jnp.exp2(arg_f) + cb_b * jnp.exp2(arg_b)).astype(BF16)
            hcols = slice(h * B_HEAD_DIM, (h + 1) * B_HEAD_DIM)
            yacc_ref[:, hcols] += jnp.dot(mm, act_ref[:, hcols], preferred_element_type=F32)
        xw = (x_g.astype(F32) * w_f[:, cols]).astype(BF16)
        upd = lax.dot_general(b_g, xw, (((0,), (0,)), ((), ())), preferred_element_type=F32)
        hf_ref[g] = hf_g * dec_f[CHUNK - 1:CHUNK, cols] + upd
        yz = yacc_ref[:, cols] * z_ref[:, cols].astype(F32)
        y_ref[:, cols] = (_rms_rows(yz) * ng_ref[:, cols]).astype(BF16)

    if emit_final:
        @pl.when(j == n_chunks - 1)
        def _():
            for g in range(B_GROUPS):
                hfin_ref[0, g * B_GROUP_CH:(g + 1) * B_GROUP_CH, :] = hf_ref[g].T


def _ssd_fwd(act, dt_raw, p, hin, dt_bias, a_log, e2f, e2b, d_skip, norm_g, h0, n_batch, n_chunks, emit_final):
    m = act.shape[0]
    has_init = h0 is not None
    blk = lambda b, j: b * n_chunks + j
    zcol = (2 * A_WIDTH + A_WIDTH) // B_WIDTH
    const = lambda b, j: (0, 0)
    in_specs = [
        pl.BlockSpec((CHUNK, B_CONV_CH), lambda b, j: (blk(b, j), 0)),
        pl.BlockSpec((CHUNK, 128), lambda b, j: (blk(b, j), 0)),
        pl.BlockSpec((CHUNK, B_WIDTH), lambda b, j: (blk(b, j), zcol)),
        pl.BlockSpec((1, B_GROUPS, B_STATE, B_GROUP_CH), lambda b, j: (blk(b, j), 0, 0, 0)),
        pl.BlockSpec((1, 128), const),
        pl.BlockSpec((1, 128), const),
        pl.BlockSpec((2 * CHUNK, B_WIDTH), const),
        pl.BlockSpec((2 * CHUNK, B_WIDTH), const),
        pl.BlockSpec((2, B_WIDTH), const),
        pl.BlockSpec((1, B_WIDTH), const),
    ]
    args = [act, dt_raw, p, hin, dt_bias, a_log, e2f, e2b, d_skip, norm_g]
    if has_init:
        in_specs.append(pl.BlockSpec((1, B_WIDTH, B_STATE), lambda b, j: (b, 0, 0)))
        args.append(h0)
    out_shape = [jax.ShapeDtypeStruct((m, B_WIDTH), BF16)]
    out_specs = [pl.BlockSpec((CHUNK, B_WIDTH), lambda b, j: (blk(b, j), 0))]
    if emit_final:
        out_shape.append(jax.ShapeDtypeStruct((n_batch, B_WIDTH, B_STATE), F32))
        out_specs.append(pl.BlockSpec((1, B_WIDTH, B_STATE), lambda b, j: (b, 0, 0)))
    return pl.pallas_call(
        functools.partial(_ssd_fwd_kernel, n_chunks=n_chunks, has_init=has_init, emit_final=emit_final),
        out_shape=tuple(out_shape),
        grid=(n_batch, n_chunks),
        in_specs=in_specs,
        out_specs=tuple(out_specs),
        scratch_shapes=[pltpu.VMEM((B_GROUPS, B_STATE, B_GROUP_CH), F32),
                        pltpu.VMEM((CHUNK, B_WIDTH), F32)],
        compiler_params=_params(("parallel", "arbitrary")),
        name="l0_ssd_fwd",
    )(*args)


def _outproj_kernel(*refs, n_in):
    acts, ws = refs[:n_in], refs[n_in:2 * n_in]
    x_ref, mod_ref, g_ref, o_ref = refs[2 * n_in:]
    acc = jnp.dot(acts[0][...], ws[0][...], preferred_element_type=F32)
    for a_ref, w_ref in zip(acts[1:], ws[1:]):
        acc = acc + jnp.dot(a_ref[...], w_ref[...], preferred_element_type=F32)
    o_ref[...] = x_ref[...] + mod_ref[0, 2:3, :] * (_rms_rows(acc) * g_ref[...])


def _outproj(acts, w, x, mod, g, rows_per_cond, tm=512):
    m = x.shape[0]
    n_in = len(acts)
    kdim = acts[0].shape[1]
    row = _mod_row_map(rows_per_cond, tm)
    in_specs = [pl.BlockSpec((tm, kdim), lambda i: (i, 0)) for _ in acts]
    in_specs += [pl.BlockSpec((kdim, D_MODEL), functools.partial(lambda i, k: (k, 0), k=k),
                              pipeline_mode=pl.Buffered(1)) for k in range(n_in)]
    in_specs += [pl.BlockSpec((tm, D_MODEL), lambda i: (i, 0)),
                 pl.BlockSpec((1, 3, D_MODEL), lambda i: (row(i), 0, 0)),
                 pl.BlockSpec((1, D_MODEL), lambda i: (0, 0))]
    return pl.pallas_call(
        functools.partial(_outproj_kernel, n_in=n_in),
        out_shape=jax.ShapeDtypeStruct((m, D_MODEL), F32),
        grid=(m // tm,),
        in_specs=in_specs,
        out_specs=pl.BlockSpec((tm, D_MODEL), lambda i: (i, 0)),
        compiler_params=_params(("parallel",)),
        name="outproj",
    )(*acts, *([w] * n_in), x, mod, g)


def _rope(x, cos, sin_signed, first_of_pair):
    swapped = jnp.where(first_of_pair, pltpu.roll(x, C_HEAD_DIM - 32, axis=1), pltpu.roll(x, 32, axis=1))
    return x * cos + swapped * sin_signed


def _inproj1_kernel(*refs, tn, use_rope, emit_kv):
    it = iter(refs)
    x_ref, mod_ref, g_ref, w_ref, qn_ref, kn_ref = next(it), next(it), next(it), next(it), next(it), next(it)
    cos_ref = next(it) if use_rope else None
    sin_ref = next(it) if use_rope else None
    p_ref = next(it)
    k_ref = next(it) if emit_kv else None
    v_ref = next(it) if emit_kv else None
    h_ref = next(it)

    j = pl.program_id(1)
    nq = C_WIDTH // tn
    nz = C_WIDTH // tn
    nk = C_KV_WIDTH // tn
    heads = tn // C_HEAD_DIM

    @pl.when(j == 0)
    def _():
        h_ref[...] = _prenorm(x_ref, mod_ref, g_ref)

    acc = jnp.dot(h_ref[...], w_ref[...], preferred_element_type=F32)

    def normed_heads(gain_ref, out_scale):
        outs = []
        if use_rope:
            lane = lax.broadcasted_iota(jnp.int32, (acc.shape[0], C_HEAD_DIM), 1)
            first = (lane // 32) % 2 == 0
        for hh in range(heads):
            xh = _rms_rows(acc[:, hh * C_HEAD_DIM:(hh + 1) * C_HEAD_DIM]) * gain_ref[...]
            rot = _rope(xh, cos_ref[...], sin_ref[...], first) if use_rope else xh
            outs.append((xh, rot * out_scale if out_scale != 1.0 else rot))
        return outs

    @pl.when(j < nq)
    def _():
        for hh, (_, qh) in enumerate(normed_heads(qn_ref, C_HEAD_DIM ** -0.5)):
            p_ref[:, hh * C_HEAD_DIM:(hh + 1) * C_HEAD_DIM] = qh.astype(BF16)

    @pl.when((j >= nq) & (j < nq + nz))
    def _():
        p_ref[...] = _silu(acc).astype(BF16)

    @pl.when((j >= nq + nz) & (j < nq + nz + nk))
    def _():
        for hh, (kh, kr) in enumerate(normed_heads(kn_ref, 1.0)):
            p_ref[:, hh * C_HEAD_DIM:(hh + 1) * C_HEAD_DIM] = kr.astype(BF16)
            if emit_kv:
                k_ref[:, hh * C_HEAD_DIM:(hh + 1) * C_HEAD_DIM] = kh

    @pl.when(j >= nq + nz + nk)
    def _():
        p_ref[...] = acc.astype(BF16)
        if emit_kv:
            v_ref[...] = acc


def _inproj1(x, mod, g, w, q_norm, k_norm, rope, rows_per_cond, emit_kv, tm=1024, tn=512):
    m = x.shape[0]
    row = _mod_row_map(rows_per_cond, tm)
    use_rope = rope is not None
    in_specs = [pl.BlockSpec((tm, D_MODEL), lambda i, j: (i, 0)),
                pl.BlockSpec((1, 3, D_MODEL), lambda i, j: (row(i), 0, 0)),
                pl.BlockSpec((1, D_MODEL), lambda i, j: (0, 0)),
                pl.BlockSpec((D_MODEL, tn), lambda i, j: (0, j)),
                pl.BlockSpec((1, C_HEAD_DIM), lambda i, j: (0, 0)),
                pl.BlockSpec((1, C_HEAD_DIM), lambda i, j: (0, 0))]
    args = [x, mod, g, w, q_norm, k_norm]
    if use_rope:
        tiles_per_seq = rope[0].shape[0] // tm
        for t in rope:
            in_specs.append(pl.BlockSpec((tm, C_HEAD_DIM), lambda i, j: (i % tiles_per_seq, 0)))
            args.append(t)
    out_shape = [jax.ShapeDtypeStruct((m, L1_IN), BF16)]
    out_specs = [pl.BlockSpec((tm, tn), lambda i, j: (i, j))]
    if emit_kv:
        kv_tile = (2 * C_WIDTH) // tn
        assert C_KV_WIDTH == tn
        out_shape += [jax.ShapeDtypeStruct((m, C_KV_WIDTH), F32)] * 2
        out_specs += [pl.BlockSpec((tm, C_KV_WIDTH), lambda i, j: (i, 0))] * 2
    return pl.pallas_call(
        functools.partial(_inproj1_kernel, tn=tn, use_rope=use_rope, emit_kv=emit_kv),
        out_shape=tuple(out_shape),
        grid=(m // tm, L1_IN // tn),
        in_specs=in_specs,
        out_specs=tuple(out_specs),
        scratch_shapes=[pltpu.VMEM((tm, D_MODEL), BF16)],
        compiler_params=_params(("parallel", "arbitrary")),
        name="l1_inproj",
    )(*args)


def _attn_kernel(*refs, n_kv_src):
    q_ref, z_ref = refs[0], refs[1]
    k_refs = refs[2:2 + n_kv_src]
    v_refs = refs[2 + n_kv_src:2 + 2 * n_kv_src]
    o_ref = refs[2 + 2 * n_kv_src]
    rep = C_HEADS // C_KV_HEADS
    nt = (((1,), (1,)), ((), ()))
    for kv in range(C_KV_HEADS):
        kcols = slice(kv * C_HEAD_DIM, (kv + 1) * C_HEAD_DIM)
        ks = [k[0, :, kcols] if len(k.shape) == 3 else k[:, kcols] for k in k_refs]
        vs = [v[0, :, kcols] if len(v.shape) == 3 else v[:, kcols] for v in v_refs]
        for e in range(rep):
            cols = slice((kv * rep + e) * C_HEAD_DIM, (kv * rep + e + 1) * C_HEAD_DIM)
            q = q_ref[:, cols]
            ss = [lax.dot_general(q, k, nt, preferred_element_type=F32) for k in ks]
            mx = jnp.max(ss[0], axis=-1, keepdims=True)
            for s in ss[1:]:
                mx = jnp.maximum(mx, jnp.max(s, axis=-1, keepdims=True))
            den = None
            o = None
            for s, v in zip(ss, vs):
                pr = jnp.exp(s - mx)
                d = jnp.sum(pr, axis=-1, keepdims=True)
                den = d if den is None else den + d
                t = jnp.dot(pr.astype(BF16), v, preferred_element_type=F32)
                o = t if o is None else o + t
            o = o * (1.0 / den) * z_ref[:, cols].astype(F32)
            o_ref[:, cols] = o.astype(BF16)


def _attention(qkvz, seq_len, n_batch, tq, ctx_k=None, ctx_v=None):
    m = qkvz.shape[0]
    qb = seq_len // tq
    kcol = (2 * C_WIDTH) // C_KV_WIDTH
    in_specs = [pl.BlockSpec((tq, C_WIDTH), lambda b, i: (b * qb + i, 0)),
                pl.BlockSpec((tq, C_WIDTH), lambda b, i: (b * qb + i, 1))]
    k_specs = [pl.BlockSpec((seq_len, C_KV_WIDTH), lambda b, i: (b, kcol))]
    v_specs = [pl.BlockSpec((seq_len, C_KV_WIDTH), lambda b, i: (b, kcol + 1))]
    k_args, v_args = [qkvz], [qkvz]
    if ctx_k is not None:
        past = ctx_k.shape[1]
        k_specs.insert(0, pl.BlockSpec((1, past, C_KV_WIDTH), lambda b, i: (b, 0, 0)))
        v_specs.insert(0, pl.BlockSpec((1, past, C_KV_WIDTH), lambda b, i: (b, 0, 0)))
        k_args.insert(0, ctx_k)
        v_args.insert(0, ctx_v)
    return pl.pallas_call(
        functools.partial(_attn_kernel, n_kv_src=len(k_args)),
        out_shape=jax.ShapeDtypeStruct((m, C_WIDTH), BF16),
        grid=(n_batch, qb),
        in_specs=in_specs + k_specs + v_specs,
        out_specs=pl.BlockSpec((tq, C_WIDTH), lambda b, i: (b * qb + i, 0)),
        compiler_params=_params(("parallel", "arbitrary")),
        name="l1_attention",
    )(qkvz, qkvz, *k_args, *v_args)


def _expand_matrix(first_row):
    e = np.zeros((2 * CHUNK, B_WIDTH), np.float32)
    ch = np.arange(B_WIDTH)
    e[first_row + ch // B_HEAD_DIM, ch] = 1.0
    e[CHUNK + first_row + ch // B_HEAD_DIM, ch] = 1.0
    return jnp.asarray(e, BF16)


def _rope_tables(n):
    t = np.arange(n)
    pos = np.stack([t // GRID_W, t % GRID_W], axis=1).astype(np.float64)
    half = C_HEAD_DIM // 2
    inv = ROPE_THETA ** (-np.arange(0, half, 2, dtype=np.float64) / half)
    ang = pos[:, :, None] * inv[None, None, :]
    cos = np.concatenate([np.cos(ang), np.cos(ang)], axis=-1).reshape(n, C_HEAD_DIM)
    sin = np.concatenate([-np.sin(ang), np.sin(ang)], axis=-1).reshape(n, C_HEAD_DIM)
    return jnp.asarray(cos, F32), jnp.asarray(sin, F32)


def _pad_lanes(x, width=128):
    return jnp.pad(x, ((0, 0), (0, width - x.shape[1])))


def kernel(x_prompt, x_sample, state_l0_ssm_fwd, state_l0_ssm_bwd, cache_l1_k, cache_l1_v, c, c_ctx, mod_w0, mod_b0, norm_pre0, norm_post0, l0_w_in, l0_v_gain, l0_w_s, l0_b_s, l0_conv_w, l0_conv_b, l0_dt_bias, l0_a_log, l0_d_skip, l0_ssm_norm, l0_w_out, mod_w1, mod_b1, norm_pre1, norm_post1, l1_w_in, l1_q_norm, l1_k_norm, l1_w_out):
    pb, pn, d = x_prompt.shape
    sb, sn, _ = x_sample.shape
    xp = x_prompt.reshape(pb * pn, d)
    xs = x_sample.reshape(sb * sn, d)
    row = lambda v: v.reshape(1, -1)

    cond = jnp.zeros((MOD_ROWS, d), F32).at[:sb].set(c).at[CTX_ROW].set(c_ctx)
    w0_main = l0_w_in[:, :L0_MAIN].astype(BF16)
    w0_dt = _pad_lanes(l0_w_in[:, L0_MAIN:]).astype(BF16)
    w0_out = l0_w_out.astype(BF16)
    ws = l0_w_s.astype(BF16)
    bs = jnp.broadcast_to(l0_b_s[:, :, None], (A_GROUPS, CHUNK, CHUNK))
    conv_w = jnp.pad(l0_conv_w, ((0, 8 - B_CONV), (0, 0)))
    dt_bias = _pad_lanes(l0_dt_bias.reshape(1, 2 * B_HEADS))
    a_log = _pad_lanes(l0_a_log.reshape(1, 2 * B_HEADS))
    d_skip = jnp.repeat(l0_d_skip, B_HEAD_DIM, axis=1)
    e2f, e2b = _expand_matrix(0), _expand_matrix(B_HEADS)
    kv0 = C_WIDTH
    z0 = C_WIDTH + 2 * C_KV_WIDTH
    w1_in = jnp.concatenate([l1_w_in[:, :kv0], l1_w_in[:, z0:], l1_w_in[:, kv0:z0]], axis=1).astype(BF16)
    w1_out = l1_w_out.astype(BF16)
    ctx_k = cache_l1_k.reshape(sb, -1, C_KV_WIDTH).astype(BF16)
    ctx_v = cache_l1_v.reshape(sb, -1, C_KV_WIDTH).astype(BF16)
    rope = _rope_tables(sn)

    mod0 = _modulation(cond, mod_w0, mod_b0)
    mod1 = _modulation(cond, mod_w1, mod_b1)

    def layer0(x, rows_per_cond, n_batch, seq, h0f, h0b, emit_final):
        n_chunks = seq // CHUNK
        p, dt_raw = _inproj0(x, mod0, row(norm_pre0), w0_main, w0_dt, rows_per_cond)
        a_out = _mixer_a(p, row(l0_v_gain), ws, bs)
        bwd = _ssd_bwd(p, dt_raw, conv_w, row(l0_conv_b), dt_bias, a_log, e2b, h0b,
                       n_batch, n_chunks, emit_final)
        fwd = _ssd_fwd(bwd[0], dt_raw, p, bwd[1], dt_bias, a_log, e2f, e2b, d_skip, row(l0_ssm_norm),
                       h0f, n_batch, n_chunks, emit_final)
        x1 = _outproj([a_out, fwd[0]], w0_out, x, mod0, row(norm_post0), rows_per_cond)
        return x1, (fwd[1] if emit_final else None), (bwd[2] if emit_final else None)

    st_shape = (sb, B_WIDTH, B_STATE)
    xp1, new_f, new_b = layer0(xp, None, pb, pn, None, None, True)
    xs1, _, _ = layer0(xs, sn, sb, sn, state_l0_ssm_fwd.reshape(st_shape),
                       state_l0_ssm_bwd.reshape(st_shape), False)

    qp, new_k, new_v = _inproj1(xp1, mod1, row(norm_pre1), w1_in, row(l1_q_norm), row(l1_k_norm),
                                None, None, True)
    op = _attention(qp, pn, pb, pn)
    yp = _outproj([op], w1_out, xp1, mod1, row(norm_post1), None)

    (qs,) = _inproj1(xs1, mod1, row(norm_pre1), w1_in, row(l1_q_norm), row(l1_k_norm),
                     rope, sn, False)
    os_ = _attention(qs, sn, sb, 256, ctx_k, ctx_v)
    ys = _outproj([os_], w1_out, xs1, mod1, row(norm_post1), sn)

    return (yp.reshape(pb, pn, d), ys.reshape(sb, sn, d),
            new_f.reshape(pb, B_HEADS, B_HEAD_DIM, B_STATE),
            new_b.reshape(pb, B_HEADS, B_HEAD_DIM, B_STATE),
            new_k.reshape(pb, pn, C_KV_HEADS, C_HEAD_DIM),
            new_v.reshape(pb, pn, C_KV_HEADS, C_HEAD_DIM))
```

```python
import functools

import numpy as np
import jax
import jax.numpy as jnp
from jax import lax
from jax.experimental import pallas as pl
from jax.experimental.pallas import tpu as pltpu

F32 = jnp.float32
BF16 = jnp.bfloat16

EPS = 1e-6
LOG2E = 1.4426950408889634
GELU_K = 0.7978845608028654

D_MODEL = 2048
CHUNK = 128
GRID_W = 64
ROPE_THETA = 10000.0
A_WIDTH = 2048
A_GROUPS = 16
B_WIDTH = 2048
B_HEAD_DIM = 64
B_HEADS = 32
B_GROUPS = 8
B_STATE = 128
B_CONV = 5
B_GN = B_GROUPS * B_STATE
B_CONV_CH = B_WIDTH + 2 * B_GN
B_GROUP_CH = B_WIDTH // B_GROUPS
L0_MAIN = 2 * A_WIDTH + A_WIDTH + B_WIDTH + B_CONV_CH
C_HEADS = 16
C_KV_HEADS = 4
C_HEAD_DIM = 128
C_WIDTH = 2048
C_KV_WIDTH = 512
L1_IN = C_WIDTH + 2 * C_KV_WIDTH + C_WIDTH

MOD_ROWS = 16
CTX_ROW = 8
V7X_VMEM_LIMIT = 56 * 1024 * 1024
SUB = 256
GAP = 16


def _silu(x):
    return x * (1.0 / (1.0 + jnp.exp(-x)))


def _silu_tanh(x):
    return x * (0.5 + 0.5 * jnp.tanh(0.5 * x))


def _softplus(x):
    return jnp.maximum(x, 0.0) + jnp.log1p(jnp.exp(-jnp.abs(x)))


def _rms_rows(x):
    return x * lax.rsqrt(jnp.mean(x * x, axis=-1, keepdims=True) + EPS)


def _params(sem, vmem=V7X_VMEM_LIMIT):
    return pltpu.CompilerParams(dimension_semantics=sem, vmem_limit_bytes=vmem)


def _mod_kernel(c_ref, w_ref, b_ref, o_ref):
    s = _silu(c_ref[...]).astype(BF16)
    o_ref[...] = jnp.dot(s, w_ref[...].astype(BF16), preferred_element_type=F32) + b_ref[...]


def _modulation(cond, w, b, tn=512):
    n = w.shape[1]
    m = pl.pallas_call(
        _mod_kernel,
        out_shape=jax.ShapeDtypeStruct((MOD_ROWS, n), F32),
        grid=(n // tn,),
        in_specs=[pl.BlockSpec((MOD_ROWS, D_MODEL), lambda j: (0, 0)),
                  pl.BlockSpec((D_MODEL, tn), lambda j: (0, j)),
                  pl.BlockSpec((1, tn), lambda j: (0, j))],
        out_specs=pl.BlockSpec((MOD_ROWS, tn), lambda j: (0, j)),
        compiler_params=_params(("arbitrary",)),
        name="modulation",
    )(cond, w, b.reshape(1, n))
    return m.reshape(MOD_ROWS, 3, D_MODEL)


def _mod_row_map(rows_per_cond, tm):
    if rows_per_cond is None:
        return lambda i: CTX_ROW
    return lambda i: (i * tm) // rows_per_cond


def _prenorm(x, mod_ref, g_ref):
    y = _rms_rows(x) * g_ref[...]
    return (y * (1.0 + mod_ref[0, 1:2, :]) + mod_ref[0, 0:1, :]).astype(BF16)


def _inproj0_kernel(xp_ref, x_ref, xn_ref, mod_ref, g_ref, w_ref, wdt_ref, cw_ref, cb_ref,
                    p_ref, dt_ref, h_ref, hg_ref, *, tm, tn, seq, n_gelu, n_gated):
    i = pl.program_id(0)
    j = pl.program_id(1)
    nseg = max(1, tm // seq)
    seg = tm // nseg
    rows_g = hg_ref.shape[0]

    @pl.when(j == 0)
    def _():
        h = _prenorm(x_ref[...], mod_ref, g_ref)
        h_ref[...] = h
        dt_ref[...] = jnp.dot(h, wdt_ref[...], preferred_element_type=F32)
        zero = jnp.zeros((GAP, D_MODEL), BF16)
        has_prev = (i * tm) % seq != 0
        has_next = ((i + 1) * tm) % seq != 0
        hg_ref[0:GAP, :] = jnp.where(has_prev, _prenorm(xp_ref[...], mod_ref, g_ref), zero)
        for q in range(nseg):
            base = GAP + q * (seg + GAP)
            hg_ref[base:base + seg, :] = h[q * seg:(q + 1) * seg]
            if q < nseg - 1:
                hg_ref[base + seg:base + seg + GAP, :] = zero
        hg_ref[rows_g - GAP:rows_g, :] = jnp.where(has_next, _prenorm(xn_ref[...], mod_ref, g_ref), zero)

    @pl.when(j < n_gated)
    def _():
        is_gelu = j < n_gelu
        a = jnp.where(is_gelu, GELU_K, 0.5).astype(F32)
        b = jnp.where(is_gelu, GELU_K * 0.044715, 0.0).astype(F32)
        for s in range(tn // SUB):
            sub = slice(s * SUB, (s + 1) * SUB)
            acc = jnp.dot(h_ref[...], w_ref[:, sub], preferred_element_type=F32)
            t = jnp.tanh(acc * (a + b * (acc * acc)))
            p_ref[:, sub] = (acc * (0.5 + 0.5 * t)).astype(BF16)

    @pl.when(j >= n_gated)
    def _():
        for s in range(tn // SUB):
            sub = slice(s * SUB, (s + 1) * SUB)
            acc = jnp.dot(hg_ref[...], w_ref[:, sub], preferred_element_type=F32)
            out = None
            for k in range(B_CONV):
                shift = (B_CONV // 2 - k) % rows_g
                tap = acc if shift == 0 else pltpu.roll(acc, shift, axis=0)
                term = tap * cw_ref[k:k + 1, sub]
                out = term if out is None else out + term
            out = _silu_tanh(out + cb_ref[:, sub]).astype(BF16)
            for q in range(nseg):
                base = GAP + q * (seg + GAP)
                p_ref[q * seg:(q + 1) * seg, sub] = out[base:base + seg]


def _inproj0(x, mod, g, w_main, w_dt, conv_w, conv_b, rows_per_cond, seq, tm=1024, tn=1024):
    m = x.shape[0]
    row = _mod_row_map(rows_per_cond, tm)
    nseg = max(1, tm // seq)
    rows_g = tm + (nseg + 1) * GAP
    n_gated = (L0_MAIN - B_CONV_CH) // tn
    gb = tm // GAP
    last_gap = m // GAP - 1
    conv_col = lambda j: jnp.maximum(j - n_gated, 0)
    kern = functools.partial(_inproj0_kernel, tm=tm, tn=tn, seq=seq, n_gelu=2 * A_WIDTH // tn, n_gated=n_gated)
    return pl.pallas_call(
        kern,
        out_shape=(jax.ShapeDtypeStruct((m, L0_MAIN), BF16),
                   jax.ShapeDtypeStruct((m, 128), F32)),
        grid=(m // tm, L0_MAIN // tn),
        in_specs=[pl.BlockSpec((GAP, D_MODEL), lambda i, j: (jnp.maximum(i * gb - 1, 0), 0)),
                  pl.BlockSpec((tm, D_MODEL), lambda i, j: (i, 0)),
                  pl.BlockSpec((GAP, D_MODEL), lambda i, j: (jnp.minimum((i + 1) * gb, last_gap), 0)),
                  pl.BlockSpec((1, 3, D_MODEL), lambda i, j: (row(i), 0, 0)),
                  pl.BlockSpec((1, D_MODEL), lambda i, j: (0, 0)),
                  pl.BlockSpec((D_MODEL, tn), lambda i, j: (0, j)),
                  pl.BlockSpec((D_MODEL, 128), lambda i, j: (0, 0)),
                  pl.BlockSpec((8, tn), lambda i, j: (0, conv_col(j))),
                  pl.BlockSpec((1, tn), lambda i, j: (0, conv_col(j)))],
        out_specs=(pl.BlockSpec((tm, tn), lambda i, j: (i, j)),
                   pl.BlockSpec((tm, 128), lambda i, j: (i, 0))),
        scratch_shapes=[pltpu.VMEM((tm, D_MODEL), BF16),
                        pltpu.VMEM((rows_g, D_MODEL), BF16)],
        compiler_params=_params(("parallel", "arbitrary")),
        name="l0_inproj",
    )(x, x, x, mod, g, w_main, w_dt, conv_w, conv_b)


P_U, P_V, P_ZA, P_ZB, P_X = 0, 1, 2, 3, 4
P_BC = 5
P_B = (L0_MAIN - 2 * B_GN) // B_GN


def _mixa_kernel(u_ref, v_ref, z_ref, vg_ref, ws_ref, bs_ref, o_ref, *, n_chunks):
    v = v_ref[...].astype(F32)
    vc = v - jnp.mean(v, axis=-1, keepdims=True)
    vn = vc * lax.rsqrt(jnp.mean(vc * vc, axis=-1, keepdims=True) + EPS) * vg_ref[...]
    vn = vn.astype(BF16)
    gd = A_WIDTH // A_GROUPS
    for c in range(n_chunks):
        rows = slice(c * CHUNK, (c + 1) * CHUNK)
        for g in range(A_GROUPS):
            cols = slice(g * gd, (g + 1) * gd)
            s = jnp.dot(ws_ref[g], vn[rows, cols], preferred_element_type=F32) + bs_ref[g]
            o = u_ref[rows, cols].astype(F32) * s * z_ref[rows, cols].astype(F32)
            o_ref[rows, cols] = o.astype(BF16)


def _mixer_a(p, v_gain, w_s, b_s, n_chunks=4):
    m = p.shape[0]
    tm = n_chunks * CHUNK
    return pl.pallas_call(
        functools.partial(_mixa_kernel, n_chunks=n_chunks),
        out_shape=jax.ShapeDtypeStruct((m, A_WIDTH), BF16),
        grid=(m // tm,),
        in_specs=[pl.BlockSpec((tm, A_WIDTH), lambda i: (i, P_U)),
                  pl.BlockSpec((tm, A_WIDTH), lambda i: (i, P_V)),
                  pl.BlockSpec((tm, A_WIDTH), lambda i: (i, P_ZA)),
                  pl.BlockSpec((1, A_WIDTH), lambda i: (0, 0)),
                  pl.BlockSpec((A_GROUPS, CHUNK, CHUNK), lambda i: (0, 0, 0)),
                  pl.BlockSpec((A_GROUPS, CHUNK, CHUNK), lambda i: (0, 0, 0))],
        out_specs=pl.BlockSpec((tm, A_WIDTH), lambda i: (i, 0)),
        compiler_params=_params(("parallel",)),
        name="l0_mixer_a",
    )(p, p, p, v_gain, w_s, b_s)


def _split2(x):
    hi = x.astype(BF16)
    lo = (x - hi.astype(F32)).astype(BF16)
    return jnp.concatenate([hi, lo], axis=1)


def _split3(x):
    hi = x.astype(BF16)
    r = x - hi.astype(F32)
    mid = r.astype(BF16)
    lo = (r - mid.astype(F32)).astype(BF16)
    return hi, mid, lo


def _tri_matmul(tri, x):
    hi, mid, lo = _split3(x)
    out = jnp.dot(tri, lo, preferred_element_type=F32)
    out = out + jnp.dot(tri, mid, preferred_element_type=F32)
    return out + jnp.dot(tri, hi, preferred_element_type=F32)


def _expand(x_split, e2):
    return jnp.dot(x_split, e2, preferred_element_type=F32)


def _ssd_small(dtraw_ref, dtb_ref, alog_ref):
    dt = _softplus(dtraw_ref[...] + dtb_ref[...])
    da = dt * (-jnp.exp(alog_ref[...]))
    r = lax.broadcasted_iota(jnp.int32, (CHUNK, CHUNK), 0)
    c = lax.broadcasted_iota(jnp.int32, (CHUNK, CHUNK), 1)
    lower = jnp.where(c <= r, 1.0, 0.0).astype(BF16)
    upper = jnp.where(c >= r, 1.0, 0.0).astype(BF16)
    cum = jnp.where(c < B_HEADS, _tri_matmul(lower, da), _tri_matmul(upper, da))
    return dt, cum, r, c


def _init_state(state_ref, h0_ref):
    for g in range(B_GROUPS):
        if h0_ref is not None:
            state_ref[g] = h0_ref[0, g * B_GROUP_CH:(g + 1) * B_GROUP_CH, :].T
        else:
            state_ref[g] = jnp.zeros((B_STATE, B_GROUP_CH), F32)


def _emit_state(hfin_ref, state_ref):
    for g in range(B_GROUPS):
        hfin_ref[0, g * B_GROUP_CH:(g + 1) * B_GROUP_CH, :] = state_ref[g].T


def _ssd_bwd_kernel(*refs, n_chunks, has_init, emit_final):
    it = iter(refs)
    x_ref, b_ref, dtraw_ref, dtb_ref, alog_ref, e2b_ref = (next(it), next(it), next(it), next(it),
                                                          next(it), next(it))
    h0_ref = next(it) if has_init else None
    hin_ref = next(it)
    hfin_ref = next(it) if emit_final else None
    hb_ref = next(it)

    j = pl.program_id(1)

    @pl.when(j == 0)
    def _():
        _init_state(hb_ref, h0_ref)

    dt, cum, _, lane = _ssd_small(dtraw_ref, dtb_ref, alog_ref)
    is_bwd = (lane >= B_HEADS) & (lane < 2 * B_HEADS)
    w_b = jnp.where(is_bwd, dt * jnp.exp(jnp.where(is_bwd, cum[0:1, :] - cum, 0.0)), 0.0)
    w_split = _split2(w_b)
    tot_split = _split2(jnp.broadcast_to(jnp.exp(cum[0:1, :]), (GAP, CHUNK)))
    for g in range(B_GROUPS):
        cols = slice(g * B_GROUP_CH, (g + 1) * B_GROUP_CH)
        e2 = e2b_ref[:, cols]
        hb_g = hb_ref[g]
        hin_ref[0, g] = hb_g.astype(BF16)
        xw = (x_ref[:, cols].astype(F32) * _expand(w_split, e2)).astype(BF16)
        upd = lax.dot_general(b_ref[:, g * B_STATE:(g + 1) * B_STATE], xw, (((0,), (0,)), ((), ())),
                              preferred_element_type=F32)
        hb_ref[g] = hb_g * _expand(tot_split, e2)[0:1, :] + upd

    if emit_final:
        @pl.when(j == n_chunks - 1)
        def _():
            _emit_state(hfin_ref, hb_ref)


def _ssd_bwd(p, dt_raw, dt_bias, a_log, e2b, h0, n_batch, n_chunks, emit_final):
    has_init = h0 is not None
    blk = lambda b, j: b * n_chunks + (n_chunks - 1 - j)
    const = lambda b, j: (0, 0)
    in_specs = [
        pl.BlockSpec((CHUNK, B_WIDTH), lambda b, j: (blk(b, j), P_X)),
        pl.BlockSpec((CHUNK, B_GN), lambda b, j: (blk(b, j), P_B)),
        pl.BlockSpec((CHUNK, 128), lambda b, j: (blk(b, j), 0)),
        pl.BlockSpec((1, 128), const),
        pl.BlockSpec((1, 128), const),
        pl.BlockSpec((2 * CHUNK, B_WIDTH), const),
    ]
    args = [p, p, dt_raw, dt_bias, a_log, e2b]
    if has_init:
        in_specs.append(pl.BlockSpec((1, B_WIDTH, B_STATE), lambda b, j: (b, 0, 0)))
        args.append(h0)
    out_shape = [jax.ShapeDtypeStruct((n_batch * n_chunks, B_GROUPS, B_STATE, B_GROUP_CH), BF16)]
    out_specs = [pl.BlockSpec((1, B_GROUPS, B_STATE, B_GROUP_CH), lambda b, j: (blk(b, j), 0, 0, 0))]
    if emit_final:
        out_shape.append(jax.ShapeDtypeStruct((n_batch, B_WIDTH, B_STATE), F32))
        out_specs.append(pl.BlockSpec((1, B_WIDTH, B_STATE), lambda b, j: (b, 0, 0)))
    return pl.pallas_call(
        functools.partial(_ssd_bwd_kernel, n_chunks=n_chunks, has_init=has_init, emit_final=emit_final),
        out_shape=tuple(out_shape),
        grid=(n_batch, n_chunks),
        in_specs=in_specs,
        out_specs=tuple(out_specs),
        scratch_shapes=[pltpu.VMEM((B_GROUPS, B_STATE, B_GROUP_CH), F32)],
        compiler_params=_params(("parallel", "arbitrary")),
        name="l0_ssd_bwd",
    )(*args)


def _ssd_fwd_kernel(*refs, n_chunks, has_init, emit_final):
    it = iter(refs)
    x_ref, bc_ref, dtraw_ref, z_ref, hin_ref = next(it), next(it), next(it), next(it), next(it)
    dtb_ref, alog_ref, e2f_ref, e2b_ref, dsk_ref, ng_ref = (next(it), next(it), next(it), next(it),
                                                           next(it), next(it))
    h0_ref = next(it) if has_init else None
    y_ref = next(it)
    hfin_ref = next(it) if emit_final else None
    hf_ref = next(it)

    j = pl.program_id(1)

    @pl.when(j == 0)
    def _():
        _init_state(hf_ref, h0_ref)

    dt, cum, r, c = _ssd_small(dtraw_ref, dtb_ref, alog_ref)
    is_fwd = c < B_HEADS
    cum2 = cum * LOG2E
    ldt2 = jnp.log(dt) * LOG2E
    r2t = (ldt2 - cum2).T
    ldt2t = ldt2.T
    dec_split = _split2(jnp.exp(cum))
    to_end = jnp.exp(jnp.where(is_fwd, cum[CHUNK - 1:CHUNK, :] - cum, 0.0))
    wf_split = _split2(jnp.where(is_fwd, dt * to_end, 0.0))

    dsk = dsk_ref[0:1, :] + dsk_ref[1:2, :]
    heads_per_group = B_HEADS // B_GROUPS
    col_head = lax.broadcasted_iota(jnp.int32, (CHUNK, B_GROUP_CH), 1) // B_HEAD_DIM

    for g in range(B_GROUPS):
        cols = slice(g * B_GROUP_CH, (g + 1) * B_GROUP_CH)
        e2f = e2f_ref[:, cols]
        dec_f = _expand(dec_split, e2f)
        dec_b = _expand(dec_split, e2b_ref[:, cols])
        b_g = bc_ref[:, g * B_STATE:(g + 1) * B_STATE]
        c_g = bc_ref[:, B_GN + g * B_STATE:B_GN + (g + 1) * B_STATE]
        cb = lax.dot_general(c_g, b_g, (((1,), (1,)), ((), ())), preferred_element_type=F32)
        cb_f = jnp.where(c <= r, cb, 0.0)
        cb_b = jnp.where(c >= r, cb, 0.0)
        hf_g = hf_ref[g]
        y_g = jnp.dot(c_g, hf_g.astype(BF16), preferred_element_type=F32) * dec_f
        y_g = y_g + jnp.dot(c_g, hin_ref[0, g], preferred_element_type=F32) * dec_b
        x_g = x_ref[:, cols]
        y_g = y_g + dsk[:, cols] * x_g.astype(F32)
        mats, x_blocks = [], []
        for e in range(heads_per_group):
            h = g * heads_per_group + e
            hb = B_HEADS + h
            arg_f = jnp.minimum(cum2[:, h:h + 1] + r2t[h:h + 1, :], ldt2t[h:h + 1, :])
            arg_b = jnp.minimum(cum2[:, hb:hb + 1] + r2t[hb:hb + 1, :], ldt2t[hb:hb + 1, :])
            mats.append((cb_f * jnp.exp2(arg_f) + cb_b * jnp.exp2(arg_b)).astype(BF16))
            x_blocks.append(jnp.where(col_head == e, x_g, jnp.zeros_like(x_g)))
        y_g = y_g + jnp.dot(jnp.concatenate(mats, axis=1), jnp.concatenate(x_blocks, axis=0),
                            preferred_element_type=F32)
        xw = (x_g.astype(F32) * _expand(wf_split, e2f)).astype(BF16)
        upd = lax.dot_general(b_g, xw, (((0,), (0,)), ((), ())), preferred_element_type=F32)
        hf_ref[g] = hf_g * dec_f[CHUNK - 1:CHUNK, :] + upd
        yz = y_g * z_ref[:, cols].astype(F32)
        y_ref[:, cols] = (_rms_rows(yz) * ng_ref[:, cols]).astype(BF16)

    if emit_final:
        @pl.when(j == n_chunks - 1)
        def _():
            _emit_state(hfin_ref, hf_ref)


def _ssd_fwd(p, dt_raw, hin, dt_bias, a_log, e2f, e2b, d_skip, norm_g, h0, n_batch, n_chunks, emit_final):
    m = p.shape[0]
    has_init = h0 is not None
    blk = lambda b, j: b * n_chunks + j
    const = lambda b, j: (0, 0)
    in_specs = [
        pl.BlockSpec((CHUNK, B_WIDTH), lambda b, j: (blk(b, j), P_X)),
        pl.BlockSpec((CHUNK, 2 * B_GN), lambda b, j: (blk(b, j), P_BC)),
        pl.BlockSpec((CHUNK, 128), lambda b, j: (blk(b, j), 0)),
        pl.BlockSpec((CHUNK, B_WIDTH), lambda b, j: (blk(b, j), P_ZB)),
        pl.BlockSpec((1, B_GROUPS, B_STATE, B_GROUP_CH), lambda b, j: (blk(b, j), 0, 0, 0)),
        pl.BlockSpec((1, 128), const),
        pl.BlockSpec((1, 128), const),
        pl.BlockSpec((2 * CHUNK, B_WIDTH), const),
        pl.BlockSpec((2 * CHUNK, B_WIDTH), const),
        pl.BlockSpec((2, B_WIDTH), const),
        pl.BlockSpec((1, B_WIDTH), const),
    ]
    args = [p, p, dt_raw, p, hin, dt_bias, a_log, e2f, e2b, d_skip, norm_g]
    if has_init:
        in_specs.append(pl.BlockSpec((1, B_WIDTH, B_STATE), lambda b, j: (b, 0, 0)))
        args.append(h0)
    out_shape = [jax.ShapeDtypeStruct((m, B_WIDTH), BF16)]
    out_specs = [pl.BlockSpec((CHUNK, B_WIDTH), lambda b, j: (blk(b, j), 0))]
    if emit_final:
        out_shape.append(jax.ShapeDtypeStruct((n_batch, B_WIDTH, B_STATE), F32))
        out_specs.append(pl.BlockSpec((1, B_WIDTH, B_STATE), lambda b, j: (b, 0, 0)))
    return pl.pallas_call(
        functools.partial(_ssd_fwd_kernel, n_chunks=n_chunks, has_init=has_init, emit_final=emit_final),
        out_shape=tuple(out_shape),
        grid=(n_batch, n_chunks),
        in_specs=in_specs,
        out_specs=tuple(out_specs),
        scratch_shapes=[pltpu.VMEM((B_GROUPS, B_STATE, B_GROUP_CH), F32)],
        compiler_params=_params(("parallel", "arbitrary")),
        name="l0_ssd_fwd",
    )(*args)


def _outproj_kernel(*refs, n_in):
    acts, ws = refs[:n_in], refs[n_in:2 * n_in]
    x_ref, mod_ref, g_ref, o_ref = refs[2 * n_in:]
    acc = jnp.dot(acts[0][...], ws[0][...], preferred_element_type=F32)
    for a_ref, w_ref in zip(acts[1:], ws[1:]):
        acc = acc + jnp.dot(a_ref[...], w_ref[...], preferred_element_type=F32)
    o_ref[...] = x_ref[...] + mod_ref[0, 2:3, :] * (_rms_rows(acc) * g_ref[...])


def _outproj(acts, w, x, mod, g, rows_per_cond, tm=512):
    m = x.shape[0]
    n_in = len(acts)
    kdim = acts[0].shape[1]
    row = _mod_row_map(rows_per_cond, tm)
    in_specs = [pl.BlockSpec((tm, kdim), lambda i: (i, 0)) for _ in acts]
    in_specs += [pl.BlockSpec((kdim, D_MODEL), functools.partial(lambda i, k: (k, 0), k=k),
                              pipeline_mode=pl.Buffered(1)) for k in range(n_in)]
    in_specs += [pl.BlockSpec((tm, D_MODEL), lambda i: (i, 0)),
                 pl.BlockSpec((1, 3, D_MODEL), lambda i: (row(i), 0, 0)),
                 pl.BlockSpec((1, D_MODEL), lambda i: (0, 0))]
    return pl.pallas_call(
        functools.partial(_outproj_kernel, n_in=n_in),
        out_shape=jax.ShapeDtypeStruct((m, D_MODEL), F32),
        grid=(m // tm,),
        in_specs=in_specs,
        out_specs=pl.BlockSpec((tm, D_MODEL), lambda i: (i, 0)),
        compiler_params=_params(("parallel",)),
        name="outproj",
    )(*acts, *([w] * n_in), x, mod, g)


def _rope(x, cos, sin_signed, first_of_pair):
    swapped = jnp.where(first_of_pair, pltpu.roll(x, C_HEAD_DIM - 32, axis=1), pltpu.roll(x, 32, axis=1))
    return x * cos + swapped * sin_signed


def _inproj1_kernel(*refs, tn, use_rope, emit_kv):
    it = iter(refs)
    x_ref, mod_ref, g_ref, w_ref, qn_ref, kn_ref = next(it), next(it), next(it), next(it), next(it), next(it)
    cos_ref = next(it) if use_rope else None
    sin_ref = next(it) if use_rope else None
    p_ref = next(it)
    k_ref = next(it) if emit_kv else None
    v_ref = next(it) if emit_kv else None
    h_ref = next(it)

    j = pl.program_id(1)
    nq = C_WIDTH // tn
    nz = C_WIDTH // tn
    tm = h_ref.shape[0]

    @pl.when(j == 0)
    def _():
        h_ref[...] = _prenorm(x_ref[...], mod_ref, g_ref)

    def sub_dot(s):
        return jnp.dot(h_ref[...], w_ref[:, s * SUB:(s + 1) * SUB], preferred_element_type=F32)

    def norm_rope(xh, gain_ref):
        xn = _rms_rows(xh) * gain_ref[...]
        if not use_rope:
            return xn, xn
        lane = lax.broadcasted_iota(jnp.int32, (tm, C_HEAD_DIM), 1)
        return xn, _rope(xn, cos_ref[...], sin_ref[...], (lane // 32) % 2 == 0)

    @pl.when(j < nq)
    def _():
        for s in range(tn // SUB):
            acc = sub_dot(s)
            for hh in range(SUB // C_HEAD_DIM):
                _, q = norm_rope(acc[:, hh * C_HEAD_DIM:(hh + 1) * C_HEAD_DIM], qn_ref)
                col = s * SUB + hh * C_HEAD_DIM
                p_ref[:, col:col + C_HEAD_DIM] = (q * (C_HEAD_DIM ** -0.5)).astype(BF16)

    @pl.when((j >= nq) & (j < nq + nz))
    def _():
        for s in range(tn // SUB):
            p_ref[:, s * SUB:(s + 1) * SUB] = _silu_tanh(sub_dot(s)).astype(BF16)

    @pl.when(j >= nq + nz)
    def _():
        for s in range(tn // SUB):
            acc = sub_dot(s)
            if s * SUB < C_KV_WIDTH:
                for hh in range(SUB // C_HEAD_DIM):
                    k_plain, k_rot = norm_rope(acc[:, hh * C_HEAD_DIM:(hh + 1) * C_HEAD_DIM], kn_ref)
                    col = s * SUB + hh * C_HEAD_DIM
                    p_ref[:, col:col + C_HEAD_DIM] = k_rot.astype(BF16)
                    if emit_kv:
                        k_ref[:, col:col + C_HEAD_DIM] = k_plain
            else:
                p_ref[:, s * SUB:(s + 1) * SUB] = acc.astype(BF16)
                if emit_kv:
                    v_ref[:, s * SUB - C_KV_WIDTH:(s + 1) * SUB - C_KV_WIDTH] = acc


def _inproj1(x, mod, g, w, q_norm, k_norm, rope, rows_per_cond, emit_kv, tm=1024, tn=1024):
    m = x.shape[0]
    assert tn == 2 * C_KV_WIDTH
    row = _mod_row_map(rows_per_cond, tm)
    use_rope = rope is not None
    in_specs = [pl.BlockSpec((tm, D_MODEL), lambda i, j: (i, 0)),
                pl.BlockSpec((1, 3, D_MODEL), lambda i, j: (row(i), 0, 0)),
                pl.BlockSpec((1, D_MODEL), lambda i, j: (0, 0)),
                pl.BlockSpec((D_MODEL, tn), lambda i, j: (0, j)),
                pl.BlockSpec((1, C_HEAD_DIM), lambda i, j: (0, 0)),
                pl.BlockSpec((1, C_HEAD_DIM), lambda i, j: (0, 0))]
    args = [x, mod, g, w, q_norm, k_norm]
    if use_rope:
        tiles_per_seq = rope[0].shape[0] // tm
        for t in rope:
            in_specs.append(pl.BlockSpec((tm, C_HEAD_DIM), lambda i, j: (i % tiles_per_seq, 0)))
            args.append(t)
    out_shape = [jax.ShapeDtypeStruct((m, L1_IN), BF16)]
    out_specs = [pl.BlockSpec((tm, tn), lambda i, j: (i, j))]
    if emit_kv:
        out_shape += [jax.ShapeDtypeStruct((m, C_KV_WIDTH), F32)] * 2
        out_specs += [pl.BlockSpec((tm, C_KV_WIDTH), lambda i, j: (i, 0))] * 2
    return pl.pallas_call(
        functools.partial(_inproj1_kernel, tn=tn, use_rope=use_rope, emit_kv=emit_kv),
        out_shape=tuple(out_shape),
        grid=(m // tm, L1_IN // tn),
        in_specs=in_specs,
        out_specs=tuple(out_specs),
        scratch_shapes=[pltpu.VMEM((tm, D_MODEL), BF16)],
        compiler_params=_params(("parallel", "arbitrary")),
        name="l1_inproj",
    )(*args)


def _attn_kernel(*refs, n_kv_src):
    q_ref, z_ref = refs[0], refs[1]
    k_refs = refs[2:2 + n_kv_src]
    v_refs = refs[2 + n_kv_src:2 + 2 * n_kv_src]
    o_ref = refs[2 + 2 * n_kv_src]
    rep = C_HEADS // C_KV_HEADS
    nt = (((1,), (1,)), ((), ()))
    for kv in range(C_KV_HEADS):
        kcols = slice(kv * C_HEAD_DIM, (kv + 1) * C_HEAD_DIM)
        ks = [k[0, :, kcols] if len(k.shape) == 3 else k[:, kcols] for k in k_refs]
        vs = [v[0, :, kcols] if len(v.shape) == 3 else v[:, kcols] for v in v_refs]
        for e in range(rep):
            cols = slice((kv * rep + e) * C_HEAD_DIM, (kv * rep + e + 1) * C_HEAD_DIM)
            q = q_ref[:, cols]
            ss = [lax.dot_general(q, k, nt, preferred_element_type=F32) for k in ks]
            mx = jnp.max(ss[0], axis=-1, keepdims=True)
            for s in ss[1:]:
                mx = jnp.maximum(mx, jnp.max(s, axis=-1, keepdims=True))
            den = None
            o = None
            for s, v in zip(ss, vs):
                pr = jnp.exp(s - mx)
                d = jnp.sum(pr, axis=-1, keepdims=True)
                den = d if den is None else den + d
                t = jnp.dot(pr.astype(BF16), v, preferred_element_type=F32)
                o = t if o is None else o + t
            o = o * (1.0 / den) * z_ref[:, cols].astype(F32)
            o_ref[:, cols] = o.astype(BF16)


def _attention(qkvz, seq_len, n_batch, tq, ctx_k=None, ctx_v=None):
    m = qkvz.shape[0]
    qb = seq_len // tq
    kcol = (2 * C_WIDTH) // C_KV_WIDTH
    in_specs = [pl.BlockSpec((tq, C_WIDTH), lambda b, i: (b * qb + i, 0)),
                pl.BlockSpec((tq, C_WIDTH), lambda b, i: (b * qb + i, 1))]
    k_specs = [pl.BlockSpec((seq_len, C_KV_WIDTH), lambda b, i: (b, kcol))]
    v_specs = [pl.BlockSpec((seq_len, C_KV_WIDTH), lambda b, i: (b, kcol + 1))]
    k_args, v_args = [qkvz], [qkvz]
    if ctx_k is not None:
        past = ctx_k.shape[1]
        k_specs.insert(0, pl.BlockSpec((1, past, C_KV_WIDTH), lambda b, i: (b, 0, 0)))
        v_specs.insert(0, pl.BlockSpec((1, past, C_KV_WIDTH), lambda b, i: (b, 0, 0)))
        k_args.insert(0, ctx_k)
        v_args.insert(0, ctx_v)
    return pl.pallas_call(
        functools.partial(_attn_kernel, n_kv_src=len(k_args)),
        out_shape=jax.ShapeDtypeStruct((m, C_WIDTH), BF16),
        grid=(n_batch, qb),
        in_specs=in_specs + k_specs + v_specs,
        out_specs=pl.BlockSpec((tq, C_WIDTH), lambda b, i: (b * qb + i, 0)),
        compiler_params=_params(("parallel", "arbitrary")),
        name="l1_attention",
    )(qkvz, qkvz, *k_args, *v_args)


def _expand_matrix(first_row):
    e = np.zeros((2 * CHUNK, B_WIDTH), np.float32)
    ch = np.arange(B_WIDTH)
    e[first_row + ch // B_HEAD_DIM, ch] = 1.0
    e[CHUNK + first_row + ch // B_HEAD_DIM, ch] = 1.0
    return jnp.asarray(e, BF16)


def _rope_tables(n):
    t = np.arange(n)
    pos = np.stack([t // GRID_W, t % GRID_W], axis=1).astype(np.float64)
    half = C_HEAD_DIM // 2
    inv = ROPE_THETA ** (-np.arange(0, half, 2, dtype=np.float64) / half)
    ang = pos[:, :, None] * inv[None, None, :]
    cos = np.concatenate([np.cos(ang), np.cos(ang)], axis=-1).reshape(n, C_HEAD_DIM)
    sin = np.concatenate([-np.sin(ang), np.sin(ang)], axis=-1).reshape(n, C_HEAD_DIM)
    return jnp.asarray(cos, F32), jnp.asarray(sin, F32)


def _pad_lanes(x, width=128):
    return jnp.pad(x, ((0, 0), (0, width - x.shape[1])))


def kernel(x_prompt, x_sample, state_l0_ssm_fwd, state_l0_ssm_bwd, cache_l1_k, cache_l1_v, c, c_ctx, mod_w0, mod_b0, norm_pre0, norm_post0, l0_w_in, l0_v_gain, l0_w_s, l0_b_s, l0_conv_w, l0_conv_b, l0_dt_bias, l0_a_log, l0_d_skip, l0_ssm_norm, l0_w_out, mod_w1, mod_b1, norm_pre1, norm_post1, l1_w_in, l1_q_norm, l1_k_norm, l1_w_out):
    pb, pn, d = x_prompt.shape
    sb, sn, _ = x_sample.shape
    xp = x_prompt.reshape(pb * pn, d)
    xs = x_sample.reshape(sb * sn, d)
    row = lambda v: v.reshape(1, -1)

    cond = jnp.zeros((MOD_ROWS, d), F32).at[:sb].set(c).at[CTX_ROW].set(c_ctx)
    w0_main = l0_w_in[:, :L0_MAIN].astype(BF16)
    w0_dt = _pad_lanes(l0_w_in[:, L0_MAIN:]).astype(BF16)
    w0_out = l0_w_out.astype(BF16)
    ws = l0_w_s.astype(BF16)
    bs = jnp.broadcast_to(l0_b_s[:, :, None], (A_GROUPS, CHUNK, CHUNK))
    conv_w = jnp.pad(l0_conv_w, ((0, 8 - B_CONV), (0, 0)))
    dt_bias = _pad_lanes(l0_dt_bias.reshape(1, 2 * B_HEADS))
    a_log = _pad_lanes(l0_a_log.reshape(1, 2 * B_HEADS))
    d_skip = jnp.repeat(l0_d_skip, B_HEAD_DIM, axis=1)
    e2f, e2b = _expand_matrix(0), _expand_matrix(B_HEADS)
    kv0 = C_WIDTH
    z0 = C_WIDTH + 2 * C_KV_WIDTH
    w1_in = jnp.concatenate([l1_w_in[:, :kv0], l1_w_in[:, z0:], l1_w_in[:, kv0:z0]], axis=1).astype(BF16)
    w1_out = l1_w_out.astype(BF16)
    ctx_k = cache_l1_k.reshape(sb, -1, C_KV_WIDTH).astype(BF16)
    ctx_v = cache_l1_v.reshape(sb, -1, C_KV_WIDTH).astype(BF16)
    rope = _rope_tables(sn)

    mod0 = _modulation(cond, mod_w0, mod_b0)
    mod1 = _modulation(cond, mod_w1, mod_b1)

    def layer0(x, rows_per_cond, n_batch, seq, h0f, h0b, emit_final):
        n_chunks = seq // CHUNK
        p, dt_raw = _inproj0(x, mod0, row(norm_pre0), w0_main, w0_dt, conv_w, row(l0_conv_b),
                             rows_per_cond, seq)
        a_out = _mixer_a(p, row(l0_v_gain), ws, bs)
        bwd = _ssd_bwd(p, dt_raw, dt_bias, a_log, e2b, h0b, n_batch, n_chunks, emit_final)
        fwd = _ssd_fwd(p, dt_raw, bwd[0], dt_bias, a_log, e2f, e2b, d_skip, row(l0_ssm_norm),
                       h0f, n_batch, n_chunks, emit_final)
        x1 = _outproj([a_out, fwd[0]], w0_out, x, mod0, row(norm_post0), rows_per_cond)
        return x1, (fwd[1] if emit_final else None), (bwd[1] if emit_final else None)

    st_shape = (sb, B_WIDTH, B_STATE)
    xp1, new_f, new_b = layer0(xp, None, pb, pn, None, None, True)
    xs1, _, _ = layer0(xs, sn, sb, sn, state_l0_ssm_fwd.reshape(st_shape),
                       state_l0_ssm_bwd.reshape(st_shape), False)

    qp, new_k, new_v = _inproj1(xp1, mod1, row(norm_pre1), w1_in, row(l1_q_norm), row(l1_k_norm),
                                None, None, True)
    op = _attention(qp, pn, pb, pn)
    yp = _outproj([op], w1_out, xp1, mod1, row(norm_post1), None)

    (qs,) = _inproj1(xs1, mod1, row(norm_pre1), w1_in, row(l1_q_norm), row(l1_k_norm),
                     rope, sn, False)
    os_ = _attention(qs, sn, sb, 256, ctx_k, ctx_v)
    ys = _outproj([os_], w1_out, xs1, mod1, row(norm_post1), sn)

    return (yp.reshape(pb, pn, d), ys.reshape(sb, sn, d),
            new_f.reshape(pb, B_HEADS, B_HEAD_DIM, B_STATE),
            new_b.reshape(pb, B_HEADS, B_HEAD_DIM, B_STATE),
            new_k.reshape(pb, pn, C_KV_HEADS, C_HEAD_DIM),
            new_v.reshape(pb, pn, C_KV_HEADS, C_HEAD_DIM))
```

```python
import functools

import numpy as np
import jax
import jax.numpy as jnp
from jax import lax
from jax.experimental import pallas as pl
from jax.experimental.pallas import tpu as pltpu

F32 = jnp.float32
BF16 = jnp.bfloat16

EPS = 1e-6
LOG2E = 1.4426950408889634
GELU_K = 0.7978845608028654

D_MODEL = 2048
CHUNK = 128
GRID_W = 64
ROPE_THETA = 10000.0
A_WIDTH = 2048
A_GROUPS = 16
B_WIDTH = 2048
B_HEAD_DIM = 64
B_HEADS = 32
B_GROUPS = 8
B_STATE = 128
B_CONV = 5
B_GN = B_GROUPS * B_STATE
B_CONV_CH = B_WIDTH + 2 * B_GN
B_GROUP_CH = B_WIDTH // B_GROUPS
L0_MAIN = 2 * A_WIDTH + A_WIDTH + B_WIDTH + B_CONV_CH
C_HEADS = 16
C_KV_HEADS = 4
C_HEAD_DIM = 128
C_WIDTH = 2048
C_KV_WIDTH = 512
L1_IN = C_WIDTH + 2 * C_KV_WIDTH + C_WIDTH

MOD_ROWS = 16
CTX_ROW = 8
V7X_VMEM_LIMIT = 56 * 1024 * 1024
SUB = 256
GAP = 16


def _silu(x):
    return x * (1.0 / (1.0 + jnp.exp(-x)))


def _silu_tanh(x):
    return x * (0.5 + 0.5 * jnp.tanh(0.5 * x))


def _softplus(x):
    return jnp.maximum(x, 0.0) + jnp.log1p(jnp.exp(-jnp.abs(x)))


def _rms_rows(x):
    return x * lax.rsqrt(jnp.mean(x * x, axis=-1, keepdims=True) + EPS)


def _params(sem, vmem=V7X_VMEM_LIMIT):
    return pltpu.CompilerParams(dimension_semantics=sem, vmem_limit_bytes=vmem)


def _mod_kernel(c_ref, w_ref, b_ref, o_ref):
    s = _silu(c_ref[...]).astype(BF16)
    o_ref[...] = jnp.dot(s, w_ref[...].astype(BF16), preferred_element_type=F32) + b_ref[...]


def _modulation(cond, w, b, tn=512):
    n = w.shape[1]
    m = pl.pallas_call(
        _mod_kernel,
        out_shape=jax.ShapeDtypeStruct((MOD_ROWS, n), F32),
        grid=(n // tn,),
        in_specs=[pl.BlockSpec((MOD_ROWS, D_MODEL), lambda j: (0, 0)),
                  pl.BlockSpec((D_MODEL, tn), lambda j: (0, j)),
                  pl.BlockSpec((1, tn), lambda j: (0, j))],
        out_specs=pl.BlockSpec((MOD_ROWS, tn), lambda j: (0, j)),
        compiler_params=_params(("arbitrary",)),
        name="modulation",
    )(cond, w, b.reshape(1, n))
    return m.reshape(MOD_ROWS, 3, D_MODEL)


def _mod_row_map(rows_per_cond, tm):
    if rows_per_cond is None:
        return lambda i: CTX_ROW
    return lambda i: (i * tm) // rows_per_cond


def _prenorm(x, mod_ref, g_ref):
    y = _rms_rows(x) * g_ref[...]
    return (y * (1.0 + mod_ref[0, 1:2, :]) + mod_ref[0, 0:1, :]).astype(BF16)


def _inproj0_kernel(xp_ref, x_ref, xn_ref, mod_ref, g_ref, w_ref, wdt_ref, cw_ref, cb_ref,
                    p_ref, dt_ref, h_ref, hg_ref, acc_ref, *, tm, tn, seq, n_gelu, n_gated):
    i = pl.program_id(0)
    j = pl.program_id(1)
    nseg = max(1, tm // seq)
    seg = tm // nseg
    rows_g = hg_ref.shape[0]

    @pl.when(j == 0)
    def _():
        h = _prenorm(x_ref[...], mod_ref, g_ref)
        h_ref[...] = h
        dt_ref[...] = jnp.dot(h, wdt_ref[...], preferred_element_type=F32)
        zero = jnp.zeros((GAP, D_MODEL), BF16)
        has_prev = (i * tm) % seq != 0
        has_next = ((i + 1) * tm) % seq != 0
        hg_ref[0:GAP, :] = jnp.where(has_prev, _prenorm(xp_ref[...], mod_ref, g_ref), zero)
        for q in range(nseg):
            base = GAP + q * (seg + GAP)
            hg_ref[base:base + seg, :] = h[q * seg:(q + 1) * seg]
            if q < nseg - 1:
                hg_ref[base + seg:base + seg + GAP, :] = zero
        hg_ref[rows_g - GAP:rows_g, :] = jnp.where(has_next, _prenorm(xn_ref[...], mod_ref, g_ref), zero)

    @pl.when(j < n_gated)
    def _():
        is_gelu = j < n_gelu
        a = jnp.where(is_gelu, GELU_K, 0.5).astype(F32)
        b = jnp.where(is_gelu, GELU_K * 0.044715, 0.0).astype(F32)
        for s in range(tn // SUB):
            sub = slice(s * SUB, (s + 1) * SUB)
            acc = jnp.dot(h_ref[...], w_ref[:, sub], preferred_element_type=F32)
            t = jnp.tanh(acc * (a + b * (acc * acc)))
            p_ref[:, sub] = (acc * (0.5 + 0.5 * t)).astype(BF16)

    @pl.when(j >= n_gated)
    def _():
        for s in range(tn // SUB):
            sub = slice(s * SUB, (s + 1) * SUB)
            buf = s % 2
            acc_ref[buf] = jnp.dot(hg_ref[...], w_ref[:, sub], preferred_element_type=F32)
            for q in range(nseg):
                base = GAP + q * (seg + GAP)
                out = None
                for k in range(B_CONV):
                    tap = acc_ref[buf, pl.ds(base - B_CONV // 2 + k, seg), :]
                    term = tap * cw_ref[k:k + 1, sub]
                    out = term if out is None else out + term
                p_ref[q * seg:(q + 1) * seg, sub] = _silu_tanh(out + cb_ref[:, sub]).astype(BF16)


def _inproj0(x, mod, g, w_main, w_dt, conv_w, conv_b, rows_per_cond, seq, tm=1024, tn=1024):
    m = x.shape[0]
    row = _mod_row_map(rows_per_cond, tm)
    nseg = max(1, tm // seq)
    rows_g = tm + (nseg + 1) * GAP
    n_gated = (L0_MAIN - B_CONV_CH) // tn
    gb = tm // GAP
    last_gap = m // GAP - 1
    conv_col = lambda j: jnp.maximum(j - n_gated, 0)
    kern = functools.partial(_inproj0_kernel, tm=tm, tn=tn, seq=seq, n_gelu=2 * A_WIDTH // tn, n_gated=n_gated)
    return pl.pallas_call(
        kern,
        out_shape=(jax.ShapeDtypeStruct((m, L0_MAIN), BF16),
                   jax.ShapeDtypeStruct((m, 128), F32)),
        grid=(m // tm, L0_MAIN // tn),
        in_specs=[pl.BlockSpec((GAP, D_MODEL), lambda i, j: (jnp.maximum(i * gb - 1, 0), 0)),
                  pl.BlockSpec((tm, D_MODEL), lambda i, j: (i, 0)),
                  pl.BlockSpec((GAP, D_MODEL), lambda i, j: (jnp.minimum((i + 1) * gb, last_gap), 0)),
                  pl.BlockSpec((1, 3, D_MODEL), lambda i, j: (row(i), 0, 0)),
                  pl.BlockSpec((1, D_MODEL), lambda i, j: (0, 0)),
                  pl.BlockSpec((D_MODEL, tn), lambda i, j: (0, j)),
                  pl.BlockSpec((D_MODEL, 128), lambda i, j: (0, 0)),
                  pl.BlockSpec((8, tn), lambda i, j: (0, conv_col(j))),
                  pl.BlockSpec((1, tn), lambda i, j: (0, conv_col(j)))],
        out_specs=(pl.BlockSpec((tm, tn), lambda i, j: (i, j)),
                   pl.BlockSpec((tm, 128), lambda i, j: (i, 0))),
        scratch_shapes=[pltpu.VMEM((tm, D_MODEL), BF16),
                        pltpu.VMEM((rows_g, D_MODEL), BF16),
                        pltpu.VMEM((2, rows_g, SUB), F32)],
        compiler_params=_params(("parallel", "arbitrary")),
        name="l0_inproj",
    )(x, x, x, mod, g, w_main, w_dt, conv_w, conv_b)


P_U, P_V, P_ZA, P_ZB, P_X = 0, 1, 2, 3, 4
P_BC = 5
P_B = (L0_MAIN - 2 * B_GN) // B_GN


def _mixa_kernel(u_ref, v_ref, z_ref, vg_ref, ws_ref, bs_ref, o_ref, *, n_chunks):
    v = v_ref[...].astype(F32)
    vc = v - jnp.mean(v, axis=-1, keepdims=True)
    vn = vc * lax.rsqrt(jnp.mean(vc * vc, axis=-1, keepdims=True) + EPS) * vg_ref[...]
    vn = vn.astype(BF16)
    gd = A_WIDTH // A_GROUPS
    for c in range(n_chunks):
        rows = slice(c * CHUNK, (c + 1) * CHUNK)
        for g in range(A_GROUPS):
            cols = slice(g * gd, (g + 1) * gd)
            s = jnp.dot(ws_ref[g], vn[rows, cols], preferred_element_type=F32) + bs_ref[g]
            o = u_ref[rows, cols].astype(F32) * s * z_ref[rows, cols].astype(F32)
            o_ref[rows, cols] = o.astype(BF16)


def _mixer_a(p, v_gain, w_s, b_s, n_chunks=4):
    m = p.shape[0]
    tm = n_chunks * CHUNK
    return pl.pallas_call(
        functools.partial(_mixa_kernel, n_chunks=n_chunks),
        out_shape=jax.ShapeDtypeStruct((m, A_WIDTH), BF16),
        grid=(m // tm,),
        in_specs=[pl.BlockSpec((tm, A_WIDTH), lambda i: (i, P_U)),
                  pl.BlockSpec((tm, A_WIDTH), lambda i: (i, P_V)),
                  pl.BlockSpec((tm, A_WIDTH), lambda i: (i, P_ZA)),
                  pl.BlockSpec((1, A_WIDTH), lambda i: (0, 0)),
                  pl.BlockSpec((A_GROUPS, CHUNK, CHUNK), lambda i: (0, 0, 0)),
                  pl.BlockSpec((A_GROUPS, CHUNK, CHUNK), lambda i: (0, 0, 0))],
        out_specs=pl.BlockSpec((tm, A_WIDTH), lambda i: (i, 0)),
        compiler_params=_params(("parallel",)),
        name="l0_mixer_a",
    )(p, p, p, v_gain, w_s, b_s)


def _split2(x):
    hi = x.astype(BF16)
    lo = (x - hi.astype(F32)).astype(BF16)
    return jnp.concatenate([hi, lo], axis=1)


def _split3(x):
    hi = x.astype(BF16)
    r = x - hi.astype(F32)
    mid = r.astype(BF16)
    lo = (r - mid.astype(F32)).astype(BF16)
    return hi, mid, lo


def _tri_matmul(tri, x):
    hi, mid, lo = _split3(x)
    out = jnp.dot(tri, lo, preferred_element_type=F32)
    out = out + jnp.dot(tri, mid, preferred_element_type=F32)
    return out + jnp.dot(tri, hi, preferred_element_type=F32)


def _expand(x_split, e2):
    return jnp.dot(x_split, e2, preferred_element_type=F32)


def _ssd_small(dt_raw, dtb_ref, alog_ref):
    dt = _softplus(dt_raw + dtb_ref[...])
    da = dt * (-jnp.exp(alog_ref[...]))
    r = lax.broadcasted_iota(jnp.int32, (CHUNK, CHUNK), 0)
    c = lax.broadcasted_iota(jnp.int32, (CHUNK, CHUNK), 1)
    lower = jnp.where(c <= r, 1.0, 0.0).astype(BF16)
    upper = jnp.where(c >= r, 1.0, 0.0).astype(BF16)
    cum = jnp.where(c < B_HEADS, _tri_matmul(lower, da), _tri_matmul(upper, da))
    return dt, cum, r, c


def _init_state(state_ref, h0_ref):
    for g in range(B_GROUPS):
        if h0_ref is not None:
            state_ref[g] = h0_ref[0, g * B_GROUP_CH:(g + 1) * B_GROUP_CH, :].T
        else:
            state_ref[g] = jnp.zeros((B_STATE, B_GROUP_CH), F32)


def _emit_state(hfin_ref, state_ref):
    for g in range(B_GROUPS):
        hfin_ref[0, g * B_GROUP_CH:(g + 1) * B_GROUP_CH, :] = state_ref[g].T


def _ssd_bwd_kernel(*refs, n_chunks, has_init, emit_final):
    it = iter(refs)
    x_ref, b_ref, dtraw_ref, dtb_ref, alog_ref, e2b_ref = (next(it), next(it), next(it), next(it),
                                                          next(it), next(it))
    h0_ref = next(it) if has_init else None
    hin_ref = next(it)
    hfin_ref = next(it) if emit_final else None
    hb_ref = next(it)

    j = pl.program_id(1)
    cps = hin_ref.shape[0]

    @pl.when(j == 0)
    def _():
        _init_state(hb_ref, h0_ref)

    xws, tots = [], []
    for lc in range(cps):
        rows = slice(lc * CHUNK, (lc + 1) * CHUNK)
        dt, cum, _, lane = _ssd_small(dtraw_ref[rows, :], dtb_ref, alog_ref)
        is_bwd = (lane >= B_HEADS) & (lane < 2 * B_HEADS)
        w_b = jnp.where(is_bwd, dt * jnp.exp(jnp.where(is_bwd, cum[0:1, :] - cum, 0.0)), 0.0)
        xws.append((x_ref[rows, :].astype(F32) * _expand(_split2(w_b), e2b_ref[...])).astype(BF16))
        tot_split = _split2(jnp.broadcast_to(jnp.exp(cum[0:1, :]), (GAP, CHUNK)))
        tots.append(_expand(tot_split, e2b_ref[...])[0:1, :])

    for g in range(B_GROUPS):
        cols = slice(g * B_GROUP_CH, (g + 1) * B_GROUP_CH)
        hb_g = hb_ref[g]
        for lc in reversed(range(cps)):
            rows = slice(lc * CHUNK, (lc + 1) * CHUNK)
            hin_ref[lc, g] = hb_g.astype(BF16)
            upd = lax.dot_general(b_ref[rows, g * B_STATE:(g + 1) * B_STATE], xws[lc][:, cols],
                                  (((0,), (0,)), ((), ())), preferred_element_type=F32)
            hb_g = hb_g * tots[lc][:, cols] + upd
        hb_ref[g] = hb_g

    if emit_final:
        @pl.when(j == pl.num_programs(1) - 1)
        def _():
            _emit_state(hfin_ref, hb_ref)


def _ssd_bwd(p, dt_raw, dt_bias, a_log, e2b, h0, n_batch, n_chunks, emit_final, cps=2):
    has_init = h0 is not None
    steps = n_chunks // cps
    rows = cps * CHUNK
    blk = lambda b, j: b * steps + (steps - 1 - j)
    const = lambda b, j: (0, 0)
    in_specs = [
        pl.BlockSpec((rows, B_WIDTH), lambda b, j: (blk(b, j), P_X)),
        pl.BlockSpec((rows, B_GN), lambda b, j: (blk(b, j), P_B)),
        pl.BlockSpec((rows, 128), lambda b, j: (blk(b, j), 0)),
        pl.BlockSpec((1, 128), const),
        pl.BlockSpec((1, 128), const),
        pl.BlockSpec((2 * CHUNK, B_WIDTH), const),
    ]
    args = [p, p, dt_raw, dt_bias, a_log, e2b]
    if has_init:
        in_specs.append(pl.BlockSpec((1, B_WIDTH, B_STATE), lambda b, j: (b, 0, 0)))
        args.append(h0)
    out_shape = [jax.ShapeDtypeStruct((n_batch * n_chunks, B_GROUPS, B_STATE, B_GROUP_CH), BF16)]
    out_specs = [pl.BlockSpec((cps, B_GROUPS, B_STATE, B_GROUP_CH), lambda b, j: (blk(b, j), 0, 0, 0))]
    if emit_final:
        out_shape.append(jax.ShapeDtypeStruct((n_batch, B_WIDTH, B_STATE), F32))
        out_specs.append(pl.BlockSpec((1, B_WIDTH, B_STATE), lambda b, j: (b, 0, 0)))
    return pl.pallas_call(
        functools.partial(_ssd_bwd_kernel, n_chunks=n_chunks, has_init=has_init, emit_final=emit_final),
        out_shape=tuple(out_shape),
        grid=(n_batch, steps),
        in_specs=in_specs,
        out_specs=tuple(out_specs),
        scratch_shapes=[pltpu.VMEM((B_GROUPS, B_STATE, B_GROUP_CH), F32)],
        compiler_params=_params(("parallel", "arbitrary")),
        name="l0_ssd_bwd",
    )(*args)


def _ssd_fwd_kernel(*refs, n_chunks, has_init, emit_final):
    it = iter(refs)
    x_ref, bc_ref, dtraw_ref, z_ref, hin_ref = next(it), next(it), next(it), next(it), next(it)
    dtb_ref, alog_ref, e2f_ref, e2b_ref, dsk_ref, ng_ref = (next(it), next(it), next(it), next(it),
                                                           next(it), next(it))
    h0_ref = next(it) if has_init else None
    y_ref = next(it)
    hfin_ref = next(it) if emit_final else None
    hf_ref = next(it)

    j = pl.program_id(1)

    @pl.when(j == 0)
    def _():
        _init_state(hf_ref, h0_ref)

    dt, cum, r, c = _ssd_small(dtraw_ref[...], dtb_ref, alog_ref)
    is_fwd = c < B_HEADS
    cum2 = cum * LOG2E
    ldt2 = jnp.log(dt) * LOG2E
    r2t = (ldt2 - cum2).T
    ldt2t = ldt2.T
    dec_split = _split2(jnp.exp(cum))
    to_end = jnp.exp(jnp.where(is_fwd, cum[CHUNK - 1:CHUNK, :] - cum, 0.0))
    wf_split = _split2(jnp.where(is_fwd, dt * to_end, 0.0))

    dsk = dsk_ref[0:1, :] + dsk_ref[1:2, :]
    heads_per_group = B_HEADS // B_GROUPS
    col_head = lax.broadcasted_iota(jnp.int32, (CHUNK, B_GROUP_CH), 1) // B_HEAD_DIM

    for g in range(B_GROUPS):
        cols = slice(g * B_GROUP_CH, (g + 1) * B_GROUP_CH)
        e2f = e2f_ref[:, cols]
        dec_f = _expand(dec_split, e2f)
        dec_b = _expand(dec_split, e2b_ref[:, cols])
        b_g = bc_ref[:, g * B_STATE:(g + 1) * B_STATE]
        c_g = bc_ref[:, B_GN + g * B_STATE:B_GN + (g + 1) * B_STATE]
        cb = lax.dot_general(c_g, b_g, (((1,), (1,)), ((), ())), preferred_element_type=F32)
        cb_f = jnp.where(c <= r, cb, 0.0)
        cb_b = jnp.where(c >= r, cb, 0.0)
        hf_g = hf_ref[g]
        y_g = jnp.dot(c_g, hf_g.astype(BF16), preferred_element_type=F32) * dec_f
        y_g = y_g + jnp.dot(c_g, hin_ref[0, g], preferred_element_type=F32) * dec_b
        x_g = x_ref[:, cols]
        y_g = y_g + dsk[:, cols] * x_g.astype(F32)
        mats, x_blocks = [], []
        for e in range(heads_per_group):
            h = g * heads_per_group + e
            hb = B_HEADS + h
            arg_f = jnp.minimum(cum2[:, h:h + 1] + r2t[h:h + 1, :], ldt2t[h:h + 1, :])
            arg_b = jnp.minimum(cum2[:, hb:hb + 1] + r2t[hb:hb + 1, :], ldt2t[hb:hb + 1, :])
            mats.append((cb_f * jnp.exp2(arg_f) + cb_b * jnp.exp2(arg_b)).astype(BF16))
            x_blocks.append(jnp.where(col_head == e, x_g, jnp.zeros_like(x_g)))
        y_g = y_g + jnp.dot(jnp.concatenate(mats, axis=1), jnp.concatenate(x_blocks, axis=0),
                            preferred_element_type=F32)
        xw = (x_g.astype(F32) * _expand(wf_split, e2f)).astype(BF16)
        upd = lax.dot_general(b_g, xw, (((0,), (0,)), ((), ())), preferred_element_type=F32)
        hf_ref[g] = hf_g * dec_f[CHUNK - 1:CHUNK, :] + upd
        yz = y_g * z_ref[:, cols].astype(F32)
        y_ref[:, cols] = (_rms_rows(yz) * ng_ref[:, cols]).astype(BF16)

    if emit_final:
        @pl.when(j == n_chunks - 1)
        def _():
            _emit_state(hfin_ref, hf_ref)


def _ssd_fwd(p, dt_raw, hin, dt_bias, a_log, e2f, e2b, d_skip, norm_g, h0, n_batch, n_chunks, emit_final):
    m = p.shape[0]
    has_init = h0 is not None
    blk = lambda b, j: b * n_chunks + j
    const = lambda b, j: (0, 0)
    in_specs = [
        pl.BlockSpec((CHUNK, B_WIDTH), lambda b, j: (blk(b, j), P_X)),
        pl.BlockSpec((CHUNK, 2 * B_GN), lambda b, j: (blk(b, j), P_BC)),
        pl.BlockSpec((CHUNK, 128), lambda b, j: (blk(b, j), 0)),
        pl.BlockSpec((CHUNK, B_WIDTH), lambda b, j: (blk(b, j), P_ZB)),
        pl.BlockSpec((1, B_GROUPS, B_STATE, B_GROUP_CH), lambda b, j: (blk(b, j), 0, 0, 0)),
        pl.BlockSpec((1, 128), const),
        pl.BlockSpec((1, 128), const),
        pl.BlockSpec((2 * CHUNK, B_WIDTH), const),
        pl.BlockSpec((2 * CHUNK, B_WIDTH), const),
        pl.BlockSpec((2, B_WIDTH), const),
        pl.BlockSpec((1, B_WIDTH), const),
    ]
    args = [p, p, dt_raw, p, hin, dt_bias, a_log, e2f, e2b, d_skip, norm_g]
    if has_init:
        in_specs.append(pl.BlockSpec((1, B_WIDTH, B_STATE), lambda b, j: (b, 0, 0)))
        args.append(h0)
    out_shape = [jax.ShapeDtypeStruct((m, B_WIDTH), BF16)]
    out_specs = [pl.BlockSpec((CHUNK, B_WIDTH), lambda b, j: (blk(b, j), 0))]
    if emit_final:
        out_shape.append(jax.ShapeDtypeStruct((n_batch, B_WIDTH, B_STATE), F32))
        out_specs.append(pl.BlockSpec((1, B_WIDTH, B_STATE), lambda b, j: (b, 0, 0)))
    return pl.pallas_call(
        functools.partial(_ssd_fwd_kernel, n_chunks=n_chunks, has_init=has_init, emit_final=emit_final),
        out_shape=tuple(out_shape),
        grid=(n_batch, n_chunks),
        in_specs=in_specs,
        out_specs=tuple(out_specs),
        scratch_shapes=[pltpu.VMEM((B_GROUPS, B_STATE, B_GROUP_CH), F32)],
        compiler_params=_params(("parallel", "arbitrary")),
        name="l0_ssd_fwd",
    )(*args)


def _outproj_kernel(*refs, n_in):
    acts, ws = refs[:n_in], refs[n_in:2 * n_in]
    x_ref, mod_ref, g_ref, o_ref = refs[2 * n_in:]
    acc = jnp.dot(acts[0][...], ws[0][...], preferred_element_type=F32)
    for a_ref, w_ref in zip(acts[1:], ws[1:]):
        acc = acc + jnp.dot(a_ref[...], w_ref[...], preferred_element_type=F32)
    o_ref[...] = x_ref[...] + mod_ref[0, 2:3, :] * (_rms_rows(acc) * g_ref[...])


def _outproj(acts, w, x, mod, g, rows_per_cond, tm=512):
    m = x.shape[0]
    n_in = len(acts)
    kdim = acts[0].shape[1]
    row = _mod_row_map(rows_per_cond, tm)
    in_specs = [pl.BlockSpec((tm, kdim), lambda i: (i, 0)) for _ in acts]
    in_specs += [pl.BlockSpec((kdim, D_MODEL), functools.partial(lambda i, k: (k, 0), k=k),
                              pipeline_mode=pl.Buffered(1)) for k in range(n_in)]
    in_specs += [pl.BlockSpec((tm, D_MODEL), lambda i: (i, 0)),
                 pl.BlockSpec((1, 3, D_MODEL), lambda i: (row(i), 0, 0)),
                 pl.BlockSpec((1, D_MODEL), lambda i: (0, 0))]
    return pl.pallas_call(
        functools.partial(_outproj_kernel, n_in=n_in),
        out_shape=jax.ShapeDtypeStruct((m, D_MODEL), F32),
        grid=(m // tm,),
        in_specs=in_specs,
        out_specs=pl.BlockSpec((tm, D_MODEL), lambda i: (i, 0)),
        compiler_params=_params(("parallel",)),
        name="outproj",
    )(*acts, *([w] * n_in), x, mod, g)


Q_SCALE = C_HEAD_DIM ** -0.5 * LOG2E


def _inproj1_kernel(*refs, tn, emit_kv):
    it = iter(refs)
    x_ref, mod_ref, g_ref, w_ref, qn_ref, kn_ref = next(it), next(it), next(it), next(it), next(it), next(it)
    p_ref = next(it)
    k_ref = next(it) if emit_kv else None
    v_ref = next(it) if emit_kv else None
    h_ref = next(it)

    j = pl.program_id(1)
    nq = C_WIDTH // tn

    @pl.when(j == 0)
    def _():
        h_ref[...] = _prenorm(x_ref[...], mod_ref, g_ref)

    def sub_dot(s):
        return jnp.dot(h_ref[...], w_ref[:, s * SUB:(s + 1) * SUB], preferred_element_type=F32)

    @pl.when(j < nq)
    def _():
        for s in range(tn // SUB):
            acc = sub_dot(s)
            for hh in range(SUB // C_HEAD_DIM):
                q = _rms_rows(acc[:, hh * C_HEAD_DIM:(hh + 1) * C_HEAD_DIM]) * (qn_ref[...] * Q_SCALE)
                col = s * SUB + hh * C_HEAD_DIM
                p_ref[:, col:col + C_HEAD_DIM] = q.astype(BF16)

    @pl.when(j == nq)
    def _():
        for s in range(tn // SUB):
            acc = sub_dot(s)
            if s * SUB < C_KV_WIDTH:
                for hh in range(SUB // C_HEAD_DIM):
                    k = _rms_rows(acc[:, hh * C_HEAD_DIM:(hh + 1) * C_HEAD_DIM]) * kn_ref[...]
                    col = s * SUB + hh * C_HEAD_DIM
                    p_ref[:, col:col + C_HEAD_DIM] = k.astype(BF16)
                    if emit_kv:
                        k_ref[:, col:col + C_HEAD_DIM] = k
            else:
                p_ref[:, s * SUB:(s + 1) * SUB] = acc.astype(BF16)
                if emit_kv:
                    v_ref[:, s * SUB - C_KV_WIDTH:(s + 1) * SUB - C_KV_WIDTH] = acc

    @pl.when(j > nq)
    def _():
        for s in range(tn // SUB):
            p_ref[:, s * SUB:(s + 1) * SUB] = _silu_tanh(sub_dot(s)).astype(BF16)


def _inproj1(x, mod, g, w, q_norm, k_norm, rows_per_cond, emit_kv, tm=1024, tn=1024):
    m = x.shape[0]
    assert tn == 2 * C_KV_WIDTH
    row = _mod_row_map(rows_per_cond, tm)
    nq = C_WIDTH // tn
    nblk = L1_IN // tn
    out_col = lambda j: jnp.where(j < nq, j, jnp.where(j == nq, nblk - 1, j - 1))
    in_specs = [pl.BlockSpec((tm, D_MODEL), lambda i, j: (i, 0)),
                pl.BlockSpec((1, 3, D_MODEL), lambda i, j: (row(i), 0, 0)),
                pl.BlockSpec((1, D_MODEL), lambda i, j: (0, 0)),
                pl.BlockSpec((D_MODEL, tn), lambda i, j: (0, j)),
                pl.BlockSpec((1, C_HEAD_DIM), lambda i, j: (0, 0)),
                pl.BlockSpec((1, C_HEAD_DIM), lambda i, j: (0, 0))]
    out_shape = [jax.ShapeDtypeStruct((m, L1_IN), BF16)]
    out_specs = [pl.BlockSpec((tm, tn), lambda i, j: (i, out_col(j)))]
    if emit_kv:
        out_shape += [jax.ShapeDtypeStruct((m, C_KV_WIDTH), F32)] * 2
        out_specs += [pl.BlockSpec((tm, C_KV_WIDTH), lambda i, j: (i, 0))] * 2
    return pl.pallas_call(
        functools.partial(_inproj1_kernel, tn=tn, emit_kv=emit_kv),
        out_shape=tuple(out_shape),
        grid=(m // tm, nblk),
        in_specs=in_specs,
        out_specs=tuple(out_specs),
        scratch_shapes=[pltpu.VMEM((tm, D_MODEL), BF16)],
        compiler_params=_params(("parallel", "arbitrary")),
        name="l1_inproj",
    )(x, mod, g, w, q_norm, k_norm)


ROPE_HEADS = 4


def _rope_kernel(x_ref, cos_ref, sin_ref, o_ref):
    rows = x_ref.shape[0]
    lane = lax.broadcasted_iota(jnp.int32, (rows, C_HEAD_DIM), 1)
    first_of_pair = (lane // 32) % 2 == 0
    for hh in range(ROPE_HEADS):
        cols = slice(hh * C_HEAD_DIM, (hh + 1) * C_HEAD_DIM)
        x = x_ref[:, cols].astype(F32)
        swapped = jnp.where(first_of_pair, pltpu.roll(x, C_HEAD_DIM - 32, axis=1), pltpu.roll(x, 32, axis=1))
        o_ref[:, cols] = (x * cos_ref[...] + swapped * sin_ref[...]).astype(BF16)


def _rope_qk(qkvz, cos, sin, tm=1024):
    m = qkvz.shape[0]
    width = ROPE_HEADS * C_HEAD_DIM
    nq = C_WIDTH // width
    k_first = (2 * C_WIDTH) // width
    col = lambda t: jnp.where(t < nq, t, k_first + t - nq)
    tiles_per_seq = cos.shape[0] // tm
    return pl.pallas_call(
        _rope_kernel,
        out_shape=jax.ShapeDtypeStruct(qkvz.shape, qkvz.dtype),
        grid=(m // tm, nq + C_KV_WIDTH // width),
        in_specs=[pl.BlockSpec((tm, width), lambda i, t: (i, col(t))),
                  pl.BlockSpec((tm, C_HEAD_DIM), lambda i, t: (i % tiles_per_seq, 0)),
                  pl.BlockSpec((tm, C_HEAD_DIM), lambda i, t: (i % tiles_per_seq, 0))],
        out_specs=pl.BlockSpec((tm, width), lambda i, t: (i, col(t))),
        input_output_aliases={0: 0},
        compiler_params=_params(("parallel", "arbitrary")),
        name="l1_rope",
    )(qkvz, cos, sin)


def _attn_kernel(*refs, has_ctx):
    it = iter(refs)
    q_ref, z_ref, kn_ref, vn_ref = next(it), next(it), next(it), next(it)
    kc_ref = next(it) if has_ctx else None
    vc_ref = next(it) if has_ctx else None
    o_ref, kk_ref, vv_ref = next(it), next(it), next(it)
    past = kc_ref.shape[1] if has_ctx else 0
    n_new = kn_ref.shape[0]
    rep = C_HEADS // C_KV_HEADS
    nt = (((1,), (1,)), ((), ()))

    @pl.when(pl.program_id(1) == 0)
    def _():
        for kv in range(C_KV_HEADS):
            kcols = slice(kv * C_HEAD_DIM, (kv + 1) * C_HEAD_DIM)
            if has_ctx:
                kk_ref[kv, 0:past, :] = kc_ref[0, :, kcols]
                vv_ref[kv, 0:past, 0:C_HEAD_DIM] = vc_ref[0, :, kcols]
            kk_ref[kv, past:past + n_new, :] = kn_ref[:, kcols]
            vv_ref[kv, past:past + n_new, 0:C_HEAD_DIM] = vn_ref[:, kcols]
            vv_ref[kv, :, C_HEAD_DIM:2 * C_HEAD_DIM] = jnp.ones((past + n_new, C_HEAD_DIM), BF16)

    for kv in range(C_KV_HEADS):
        for e in range(rep):
            cols = slice((kv * rep + e) * C_HEAD_DIM, (kv * rep + e + 1) * C_HEAD_DIM)
            s = lax.dot_general(q_ref[:, cols], kk_ref[kv], nt, preferred_element_type=F32)
            mx = jnp.max(s, axis=-1, keepdims=True)
            o = jnp.dot(jnp.exp2(s - mx).astype(BF16), vv_ref[kv], preferred_element_type=F32)
            out = o[:, :C_HEAD_DIM] * (1.0 / o[:, C_HEAD_DIM:]) * z_ref[:, cols].astype(F32)
            o_ref[:, cols] = out.astype(BF16)


def _attention(qkvz, seq_len, n_batch, tq, ctx_k=None, ctx_v=None):
    m = qkvz.shape[0]
    qb = seq_len // tq
    kcol = (2 * C_WIDTH) // C_KV_WIDTH
    has_ctx = ctx_k is not None
    past = ctx_k.shape[1] if has_ctx else 0
    in_specs = [pl.BlockSpec((tq, C_WIDTH), lambda b, i: (b * qb + i, 0)),
                pl.BlockSpec((tq, C_WIDTH), lambda b, i: (b * qb + i, 1)),
                pl.BlockSpec((seq_len, C_KV_WIDTH), lambda b, i: (b, kcol)),
                pl.BlockSpec((seq_len, C_KV_WIDTH), lambda b, i: (b, kcol + 1))]
    args = [qkvz, qkvz, qkvz, qkvz]
    if has_ctx:
        in_specs += [pl.BlockSpec((1, past, C_KV_WIDTH), lambda b, i: (b, 0, 0))] * 2
        args += [ctx_k, ctx_v]
    return pl.pallas_call(
        functools.partial(_attn_kernel, has_ctx=has_ctx),
        out_shape=jax.ShapeDtypeStruct((m, C_WIDTH), BF16),
        grid=(n_batch, qb),
        in_specs=in_specs,
        out_specs=pl.BlockSpec((tq, C_WIDTH), lambda b, i: (b * qb + i, 0)),
        scratch_shapes=[pltpu.VMEM((C_KV_HEADS, past + seq_len, C_HEAD_DIM), BF16),
                        pltpu.VMEM((C_KV_HEADS, past + seq_len, 2 * C_HEAD_DIM), BF16)],
        compiler_params=_params(("parallel", "arbitrary")),
        name="l1_attention",
    )(*args)


def _expand_matrix(first_row):
    e = np.zeros((2 * CHUNK, B_WIDTH), np.float32)
    ch = np.arange(B_WIDTH)
    e[first_row + ch // B_HEAD_DIM, ch] = 1.0
    e[CHUNK + first_row + ch // B_HEAD_DIM, ch] = 1.0
    return jnp.asarray(e, BF16)


def _rope_tables(n):
    t = np.arange(n)
    pos = np.stack([t // GRID_W, t % GRID_W], axis=1).astype(np.float64)
    half = C_HEAD_DIM // 2
    inv = ROPE_THETA ** (-np.arange(0, half, 2, dtype=np.float64) / half)
    ang = pos[:, :, None] * inv[None, None, :]
    cos = np.concatenate([np.cos(ang), np.cos(ang)], axis=-1).reshape(n, C_HEAD_DIM)
    sin = np.concatenate([-np.sin(ang), np.sin(ang)], axis=-1).reshape(n, C_HEAD_DIM)
    return jnp.asarray(cos, F32), jnp.asarray(sin, F32)


def _pad_lanes(x, width=128):
    return jnp.pad(x, ((0, 0), (0, width - x.shape[1])))


def kernel(x_prompt, x_sample, state_l0_ssm_fwd, state_l0_ssm_bwd, cache_l1_k, cache_l1_v, c, c_ctx, mod_w0, mod_b0, norm_pre0, norm_post0, l0_w_in, l0_v_gain, l0_w_s, l0_b_s, l0_conv_w, l0_conv_b, l0_dt_bias, l0_a_log, l0_d_skip, l0_ssm_norm, l0_w_out, mod_w1, mod_b1, norm_pre1, norm_post1, l1_w_in, l1_q_norm, l1_k_norm, l1_w_out):
    pb, pn, d = x_prompt.shape
    sb, sn, _ = x_sample.shape
    xp = x_prompt.reshape(pb * pn, d)
    xs = x_sample.reshape(sb * sn, d)
    row = lambda v: v.reshape(1, -1)

    cond = jnp.zeros((MOD_ROWS, d), F32).at[:sb].set(c).at[CTX_ROW].set(c_ctx)
    w0_main = l0_w_in.astype(BF16)
    w0_dt = _pad_lanes(l0_w_in[:, L0_MAIN:]).astype(BF16)
    w0_out = l0_w_out.astype(BF16)
    ws = l0_w_s.astype(BF16)
    bs = jnp.broadcast_to(l0_b_s[:, :, None], (A_GROUPS, CHUNK, CHUNK))
    conv_w = jnp.pad(l0_conv_w, ((0, 8 - B_CONV), (0, 0)))
    dt_bias = _pad_lanes(l0_dt_bias.reshape(1, 2 * B_HEADS))
    a_log = _pad_lanes(l0_a_log.reshape(1, 2 * B_HEADS))
    d_skip = jnp.repeat(l0_d_skip, B_HEAD_DIM, axis=1)
    e2f, e2b = _expand_matrix(0), _expand_matrix(B_HEADS)
    w1_in = l1_w_in.astype(BF16)
    w1_out = l1_w_out.astype(BF16)
    ctx_k = cache_l1_k.reshape(sb, -1, C_KV_WIDTH).astype(BF16)
    ctx_v = cache_l1_v.reshape(sb, -1, C_KV_WIDTH).astype(BF16)
    rope = _rope_tables(sn)

    mod0 = _modulation(cond, mod_w0, mod_b0)
    mod1 = _modulation(cond, mod_w1, mod_b1)

    def layer0(x, rows_per_cond, n_batch, seq, h0f, h0b, emit_final):
        n_chunks = seq // CHUNK
        p, dt_raw = _inproj0(x, mod0, row(norm_pre0), w0_main, w0_dt, conv_w, row(l0_conv_b),
                             rows_per_cond, seq)
        a_out = _mixer_a(p, row(l0_v_gain), ws, bs)
        bwd = _ssd_bwd(p, dt_raw, dt_bias, a_log, e2b, h0b, n_batch, n_chunks, emit_final,
                       cps=min(4, n_chunks))
        fwd = _ssd_fwd(p, dt_raw, bwd[0], dt_bias, a_log, e2f, e2b, d_skip, row(l0_ssm_norm),
                       h0f, n_batch, n_chunks, emit_final)
        x1 = _outproj([a_out, fwd[0]], w0_out, x, mod0, row(norm_post0), rows_per_cond)
        return x1, (fwd[1] if emit_final else None), (bwd[1] if emit_final else None)

    st_shape = (sb, B_WIDTH, B_STATE)
    xp1, new_f, new_b = layer0(xp, None, pb, pn, None, None, True)
    xs1, _, _ = layer0(xs, sn, sb, sn, state_l0_ssm_fwd.reshape(st_shape),
                       state_l0_ssm_bwd.reshape(st_shape), False)

    qp, new_k, new_v = _inproj1(xp1, mod1, row(norm_pre1), w1_in, row(l1_q_norm), row(l1_k_norm),
                                None, True)
    op = _attention(qp, pn, pb, pn)
    yp = _outproj([op], w1_out, xp1, mod1, row(norm_post1), None)

    (qs,) = _inproj1(xs1, mod1, row(norm_pre1), w1_in, row(l1_q_norm), row(l1_k_norm), sn, False)
    qs = _rope_qk(qs, *rope)
    os_ = _attention(qs, sn, sb, min(sn, 512), ctx_k, ctx_v)
    ys = _outproj([os_], w1_out, xs1, mod1, row(norm_post1), sn)

    return (yp.reshape(pb, pn, d), ys.reshape(sb, sn, d),
            new_f.reshape(pb, B_HEADS, B_HEAD_DIM, B_STATE),
            new_b.reshape(pb, B_HEADS, B_HEAD_DIM, B_STATE),
            new_k.reshape(pb, pn, C_KV_HEADS, C_HEAD_DIM),
            new_v.reshape(pb, pn, C_KV_HEADS, C_HEAD_DIM))
```

```python
import functools

import numpy as np
import jax
import jax.numpy as jnp
from jax import lax
from jax.experimental import pallas as pl
from jax.experimental.pallas import tpu as pltpu

F32 = jnp.float32
BF16 = jnp.bfloat16

EPS = 1e-6
LOG2E = 1.4426950408889634
GELU_K = 0.7978845608028654

D_MODEL = 2048
CHUNK = 128
GRID_W = 64
ROPE_THETA = 10000.0
A_WIDTH = 2048
A_GROUPS = 16
B_WIDTH = 2048
B_HEAD_DIM = 64
B_HEADS = 32
B_GROUPS = 8
B_STATE = 128
B_CONV = 5
B_GN = B_GROUPS * B_STATE
B_CONV_CH = B_WIDTH + 2 * B_GN
B_GROUP_CH = B_WIDTH // B_GROUPS
L0_MAIN = 2 * A_WIDTH + A_WIDTH + B_WIDTH + B_CONV_CH
C_HEADS = 16
C_KV_HEADS = 4
C_HEAD_DIM = 128
C_WIDTH = 2048
C_KV_WIDTH = 512
L1_IN = C_WIDTH + 2 * C_KV_WIDTH + C_WIDTH

MOD_ROWS = 16
CTX_ROW = 8
V7X_VMEM_LIMIT = 56 * 1024 * 1024
SUB = 256
GAP = 16


def _silu(x):
    return x * (1.0 / (1.0 + jnp.exp(-x)))


def _silu_tanh(x):
    return x * (0.5 + 0.5 * jnp.tanh(0.5 * x))


def _softplus(x):
    return jnp.maximum(x, 0.0) + jnp.log1p(jnp.exp(-jnp.abs(x)))


def _rms_rows(x):
    return x * lax.rsqrt(jnp.mean(x * x, axis=-1, keepdims=True) + EPS)


def _params(sem, vmem=V7X_VMEM_LIMIT):
    return pltpu.CompilerParams(dimension_semantics=sem, vmem_limit_bytes=vmem)


def _mod_kernel(c_ref, w_ref, b_ref, o_ref):
    s = _silu(c_ref[...]).astype(BF16)
    o_ref[...] = jnp.dot(s, w_ref[...].astype(BF16), preferred_element_type=F32) + b_ref[...]


def _modulation(cond, w, b, tn=512):
    n = w.shape[1]
    m = pl.pallas_call(
        _mod_kernel,
        out_shape=jax.ShapeDtypeStruct((MOD_ROWS, n), F32),
        grid=(n // tn,),
        in_specs=[pl.BlockSpec((MOD_ROWS, D_MODEL), lambda j: (0, 0)),
                  pl.BlockSpec((D_MODEL, tn), lambda j: (0, j)),
                  pl.BlockSpec((1, tn), lambda j: (0, j))],
        out_specs=pl.BlockSpec((MOD_ROWS, tn), lambda j: (0, j)),
        compiler_params=_params(("arbitrary",)),
        name="modulation",
    )(cond, w, b.reshape(1, n))
    return m.reshape(MOD_ROWS, 3, D_MODEL)


def _mod_row_map(rows_per_cond, tm):
    if rows_per_cond is None:
        return lambda i: CTX_ROW
    return lambda i: (i * tm) // rows_per_cond


def _prenorm(x, mod_ref, g_ref):
    y = _rms_rows(x) * g_ref[...]
    return (y * (1.0 + mod_ref[0, 1:2, :]) + mod_ref[0, 0:1, :]).astype(BF16)


def _inproj0_kernel(xp_ref, x_ref, xn_ref, mod_ref, g_ref, w_ref, wdt_ref, cw_ref, cb_ref,
                    p_ref, dt_ref, h_ref, hg_ref, acc_ref, *, tm, tn, seq, n_gelu, n_gated):
    i = pl.program_id(0)
    j = pl.program_id(1)
    nseg = max(1, tm // seq)
    seg = tm // nseg
    rows_g = hg_ref.shape[0]

    @pl.when(j == 0)
    def _():
        h = _prenorm(x_ref[...], mod_ref, g_ref)
        h_ref[...] = h
        dt_ref[...] = jnp.dot(h, wdt_ref[...], preferred_element_type=F32)
        zero = jnp.zeros((GAP, D_MODEL), BF16)
        has_prev = (i * tm) % seq != 0
        has_next = ((i + 1) * tm) % seq != 0
        hg_ref[0:GAP, :] = jnp.where(has_prev, _prenorm(xp_ref[...], mod_ref, g_ref), zero)
        for q in range(nseg):
            base = GAP + q * (seg + GAP)
            hg_ref[base:base + seg, :] = h[q * seg:(q + 1) * seg]
            if q < nseg - 1:
                hg_ref[base + seg:base + seg + GAP, :] = zero
        hg_ref[rows_g - GAP:rows_g, :] = jnp.where(has_next, _prenorm(xn_ref[...], mod_ref, g_ref), zero)

    @pl.when(j < n_gated)
    def _():
        is_gelu = j < n_gelu
        a = jnp.where(is_gelu, GELU_K, 0.5).astype(F32)
        b = jnp.where(is_gelu, GELU_K * 0.044715, 0.0).astype(F32)
        for s in range(tn // SUB):
            sub = slice(s * SUB, (s + 1) * SUB)
            acc = jnp.dot(h_ref[...], w_ref[:, sub], preferred_element_type=F32)
            t = jnp.tanh(acc * (a + b * (acc * acc)))
            p_ref[:, sub] = (acc * (0.5 + 0.5 * t)).astype(BF16)

    @pl.when(j >= n_gated)
    def _():
        for s in range(tn // SUB):
            sub = slice(s * SUB, (s + 1) * SUB)
            buf = s % 2
            acc_ref[buf] = jnp.dot(hg_ref[...], w_ref[:, sub], preferred_element_type=F32)
            for q in range(nseg):
                base = GAP + q * (seg + GAP)
                out = None
                for k in range(B_CONV):
                    tap = acc_ref[buf, pl.ds(base - B_CONV // 2 + k, seg), :]
                    term = tap * cw_ref[k:k + 1, sub]
                    out = term if out is None else out + term
                p_ref[q * seg:(q + 1) * seg, sub] = _silu_tanh(out + cb_ref[:, sub]).astype(BF16)


def _inproj0(x, mod, g, w_main, w_dt, conv_w, conv_b, rows_per_cond, seq, tm=1024, tn=1024):
    m = x.shape[0]
    row = _mod_row_map(rows_per_cond, tm)
    nseg = max(1, tm // seq)
    rows_g = tm + (nseg + 1) * GAP
    n_gated = (L0_MAIN - B_CONV_CH) // tn
    gb = tm // GAP
    last_gap = m // GAP - 1
    conv_col = lambda j: jnp.maximum(j - n_gated, 0)
    kern = functools.partial(_inproj0_kernel, tm=tm, tn=tn, seq=seq, n_gelu=2 * A_WIDTH // tn, n_gated=n_gated)
    return pl.pallas_call(
        kern,
        out_shape=(jax.ShapeDtypeStruct((m, L0_MAIN), BF16),
                   jax.ShapeDtypeStruct((m, 128), F32)),
        grid=(m // tm, L0_MAIN // tn),
        in_specs=[pl.BlockSpec((GAP, D_MODEL), lambda i, j: (jnp.maximum(i * gb - 1, 0), 0)),
                  pl.BlockSpec((tm, D_MODEL), lambda i, j: (i, 0)),
                  pl.BlockSpec((GAP, D_MODEL), lambda i, j: (jnp.minimum((i + 1) * gb, last_gap), 0)),
                  pl.BlockSpec((1, 3, D_MODEL), lambda i, j: (row(i), 0, 0)),
                  pl.BlockSpec((1, D_MODEL), lambda i, j: (0, 0)),
                  pl.BlockSpec((D_MODEL, tn), lambda i, j: (0, j)),
                  pl.BlockSpec((D_MODEL, 128), lambda i, j: (0, 0)),
                  pl.BlockSpec((8, tn), lambda i, j: (0, conv_col(j))),
                  pl.BlockSpec((1, tn), lambda i, j: (0, conv_col(j)))],
        out_specs=(pl.BlockSpec((tm, tn), lambda i, j: (i, j)),
                   pl.BlockSpec((tm, 128), lambda i, j: (i, 0))),
        scratch_shapes=[pltpu.VMEM((tm, D_MODEL), BF16),
                        pltpu.VMEM((rows_g, D_MODEL), BF16),
                        pltpu.VMEM((2, rows_g, SUB), F32)],
        compiler_params=_params(("parallel", "arbitrary")),
        name="l0_inproj",
    )(x, x, x, mod, g, w_main, w_dt, conv_w, conv_b)


P_U, P_V, P_ZA, P_ZB, P_X = 0, 1, 2, 3, 4
P_BC = 5
P_B = (L0_MAIN - 2 * B_GN) // B_GN


def _mixa_kernel(u_ref, v_ref, z_ref, vg_ref, ws_ref, bs_ref, o_ref, *, n_chunks):
    v = v_ref[...].astype(F32)
    vc = v - jnp.mean(v, axis=-1, keepdims=True)
    vn = vc * lax.rsqrt(jnp.mean(vc * vc, axis=-1, keepdims=True) + EPS) * vg_ref[...]
    vn = vn.astype(BF16)
    gd = A_WIDTH // A_GROUPS
    for c in range(n_chunks):
        rows = slice(c * CHUNK, (c + 1) * CHUNK)
        for g in range(A_GROUPS):
            cols = slice(g * gd, (g + 1) * gd)
            s = jnp.dot(ws_ref[g], vn[rows, cols], preferred_element_type=F32) + bs_ref[g]
            o = u_ref[rows, cols].astype(F32) * s * z_ref[rows, cols].astype(F32)
            o_ref[rows, cols] = o.astype(BF16)


def _mixer_a(p, v_gain, w_s, b_s, n_chunks=4):
    m = p.shape[0]
    tm = n_chunks * CHUNK
    return pl.pallas_call(
        functools.partial(_mixa_kernel, n_chunks=n_chunks),
        out_shape=jax.ShapeDtypeStruct((m, A_WIDTH), BF16),
        grid=(m // tm,),
        in_specs=[pl.BlockSpec((tm, A_WIDTH), lambda i: (i, P_U)),
                  pl.BlockSpec((tm, A_WIDTH), lambda i: (i, P_V)),
                  pl.BlockSpec((tm, A_WIDTH), lambda i: (i, P_ZA)),
                  pl.BlockSpec((1, A_WIDTH), lambda i: (0, 0)),
                  pl.BlockSpec((A_GROUPS, CHUNK, CHUNK), lambda i: (0, 0, 0)),
                  pl.BlockSpec((A_GROUPS, CHUNK, CHUNK), lambda i: (0, 0, 0))],
        out_specs=pl.BlockSpec((tm, A_WIDTH), lambda i: (i, 0)),
        compiler_params=_params(("parallel",)),
        name="l0_mixer_a",
    )(p, p, p, v_gain, w_s, b_s)


def _split2(x):
    hi = x.astype(BF16)
    lo = (x - hi.astype(F32)).astype(BF16)
    return jnp.concatenate([hi, lo], axis=1)


def _split3(x):
    hi = x.astype(BF16)
    r = x - hi.astype(F32)
    mid = r.astype(BF16)
    lo = (r - mid.astype(F32)).astype(BF16)
    return hi, mid, lo


def _tri_matmul(tri, x):
    hi, mid, lo = _split3(x)
    out = jnp.dot(tri, lo, preferred_element_type=F32)
    out = out + jnp.dot(tri, mid, preferred_element_type=F32)
    return out + jnp.dot(tri, hi, preferred_element_type=F32)


def _expand(x_split, e2):
    return jnp.dot(x_split, e2, preferred_element_type=F32)


def _ssd_small(dt_raw, dtb_ref, alog_ref):
    dt = _softplus(dt_raw + dtb_ref[...])
    da = dt * (-jnp.exp(alog_ref[...]))
    r = lax.broadcasted_iota(jnp.int32, (CHUNK, CHUNK), 0)
    c = lax.broadcasted_iota(jnp.int32, (CHUNK, CHUNK), 1)
    lower = jnp.where(c <= r, 1.0, 0.0).astype(BF16)
    upper = jnp.where(c >= r, 1.0, 0.0).astype(BF16)
    cum = jnp.where(c < B_HEADS, _tri_matmul(lower, da), _tri_matmul(upper, da))
    return dt, cum, r, c


def _init_state(state_ref, h0_ref):
    for g in range(B_GROUPS):
        if h0_ref is not None:
            state_ref[g] = h0_ref[0, g * B_GROUP_CH:(g + 1) * B_GROUP_CH, :].T
        else:
            state_ref[g] = jnp.zeros((B_STATE, B_GROUP_CH), F32)


def _emit_state(hfin_ref, state_ref):
    for g in range(B_GROUPS):
        hfin_ref[0, g * B_GROUP_CH:(g + 1) * B_GROUP_CH, :] = state_ref[g].T


def _ssd_bwd_kernel(*refs, n_chunks, has_init, emit_final):
    it = iter(refs)
    x_ref, b_ref, dtraw_ref, dtb_ref, alog_ref, e2b_ref = (next(it), next(it), next(it), next(it),
                                                          next(it), next(it))
    h0_ref = next(it) if has_init else None
    hin_ref, sm_ref, smb_ref = next(it), next(it), next(it)
    hfin_ref = next(it) if emit_final else None
    hb_ref = next(it)

    j = pl.program_id(1)
    cps = hin_ref.shape[0]

    @pl.when(j == 0)
    def _():
        _init_state(hb_ref, h0_ref)

    xws, tots = [], []
    for lc in range(cps):
        rows = slice(lc * CHUNK, (lc + 1) * CHUNK)
        dt, cum, _, lane = _ssd_small(dtraw_ref[rows, :], dtb_ref, alog_ref)
        is_bwd = (lane >= B_HEADS) & (lane < 2 * B_HEADS)
        w_b = jnp.where(is_bwd, dt * jnp.exp(jnp.where(is_bwd, cum[0:1, :] - cum, 0.0)), 0.0)
        xws.append((x_ref[rows, :].astype(F32) * _expand(_split2(w_b), e2b_ref[...])).astype(BF16))
        tot_split = _split2(jnp.broadcast_to(jnp.exp(cum[0:1, :]), (GAP, CHUNK)))
        tots.append(_expand(tot_split, e2b_ref[...])[0:1, :])
        cum2 = cum * LOG2E
        ldt2 = jnp.log(dt) * LOG2E
        sm_ref[rows, 0:CHUNK] = cum2
        sm_ref[rows, CHUNK:2 * CHUNK] = (ldt2 - cum2).T
        sm_ref[rows, 2 * CHUNK:3 * CHUNK] = ldt2.T
        is_fwd = lane < B_HEADS
        to_end = jnp.exp(jnp.where(is_fwd, cum[CHUNK - 1:CHUNK, :] - cum, 0.0))
        smb_ref[rows, 0:2 * CHUNK] = _split2(jnp.exp(cum))
        smb_ref[rows, 2 * CHUNK:4 * CHUNK] = _split2(jnp.where(is_fwd, dt * to_end, 0.0))

    for g in range(B_GROUPS):
        cols = slice(g * B_GROUP_CH, (g + 1) * B_GROUP_CH)
        hb_g = hb_ref[g]
        for lc in reversed(range(cps)):
            rows = slice(lc * CHUNK, (lc + 1) * CHUNK)
            hin_ref[lc, g] = hb_g.astype(BF16)
            upd = lax.dot_general(b_ref[rows, g * B_STATE:(g + 1) * B_STATE], xws[lc][:, cols],
                                  (((0,), (0,)), ((), ())), preferred_element_type=F32)
            hb_g = hb_g * tots[lc][:, cols] + upd
        hb_ref[g] = hb_g

    if emit_final:
        @pl.when(j == pl.num_programs(1) - 1)
        def _():
            _emit_state(hfin_ref, hb_ref)


def _ssd_bwd(p, dt_raw, dt_bias, a_log, e2b, h0, n_batch, n_chunks, emit_final, cps=2):
    has_init = h0 is not None
    steps = n_chunks // cps
    rows = cps * CHUNK
    blk = lambda b, j: b * steps + (steps - 1 - j)
    const = lambda b, j: (0, 0)
    in_specs = [
        pl.BlockSpec((rows, B_WIDTH), lambda b, j: (blk(b, j), P_X)),
        pl.BlockSpec((rows, B_GN), lambda b, j: (blk(b, j), P_B)),
        pl.BlockSpec((rows, 128), lambda b, j: (blk(b, j), 0)),
        pl.BlockSpec((1, 128), const),
        pl.BlockSpec((1, 128), const),
        pl.BlockSpec((2 * CHUNK, B_WIDTH), const),
    ]
    args = [p, p, dt_raw, dt_bias, a_log, e2b]
    if has_init:
        in_specs.append(pl.BlockSpec((1, B_WIDTH, B_STATE), lambda b, j: (b, 0, 0)))
        args.append(h0)
    m = n_batch * n_chunks * CHUNK
    out_shape = [jax.ShapeDtypeStruct((n_batch * n_chunks, B_GROUPS, B_STATE, B_GROUP_CH), BF16),
                 jax.ShapeDtypeStruct((m, 3 * CHUNK), F32),
                 jax.ShapeDtypeStruct((m, 4 * CHUNK), BF16)]
    out_specs = [pl.BlockSpec((cps, B_GROUPS, B_STATE, B_GROUP_CH), lambda b, j: (blk(b, j), 0, 0, 0)),
                 pl.BlockSpec((rows, 3 * CHUNK), lambda b, j: (blk(b, j), 0)),
                 pl.BlockSpec((rows, 4 * CHUNK), lambda b, j: (blk(b, j), 0))]
    if emit_final:
        out_shape.append(jax.ShapeDtypeStruct((n_batch, B_WIDTH, B_STATE), F32))
        out_specs.append(pl.BlockSpec((1, B_WIDTH, B_STATE), lambda b, j: (b, 0, 0)))
    return pl.pallas_call(
        functools.partial(_ssd_bwd_kernel, n_chunks=n_chunks, has_init=has_init, emit_final=emit_final),
        out_shape=tuple(out_shape),
        grid=(n_batch, steps),
        in_specs=in_specs,
        out_specs=tuple(out_specs),
        scratch_shapes=[pltpu.VMEM((B_GROUPS, B_STATE, B_GROUP_CH), F32)],
        compiler_params=_params(("parallel", "arbitrary")),
        name="l0_ssd_bwd",
    )(*args)


def _ssd_fwd_kernel(*refs, n_chunks, has_init, emit_final):
    it = iter(refs)
    x_ref, bc_ref, sm_ref, smb_ref, z_ref, hin_ref = (next(it), next(it), next(it), next(it),
                                                      next(it), next(it))
    e2f_ref, e2b_ref, dsk_ref, ng_ref = next(it), next(it), next(it), next(it)
    h0_ref = next(it) if has_init else None
    y_ref = next(it)
    hfin_ref = next(it) if emit_final else None
    hf_ref = next(it)

    j = pl.program_id(1)

    @pl.when(j == 0)
    def _():
        _init_state(hf_ref, h0_ref)

    cps = hin_ref.shape[0]
    r = lax.broadcasted_iota(jnp.int32, (CHUNK, CHUNK), 0)
    c = lax.broadcasted_iota(jnp.int32, (CHUNK, CHUNK), 1)
    dsk = dsk_ref[0:1, :] + dsk_ref[1:2, :]
    heads_per_group = B_HEADS // B_GROUPS
    col_head = lax.broadcasted_iota(jnp.int32, (CHUNK, B_GROUP_CH), 1) // B_HEAD_DIM

    for g in range(B_GROUPS):
        cols = slice(g * B_GROUP_CH, (g + 1) * B_GROUP_CH)
        e2f = e2f_ref[:, cols]
        e2b = e2b_ref[:, cols]
        hf_g = hf_ref[g]
        for lc in range(cps):
            rows = slice(lc * CHUNK, (lc + 1) * CHUNK)
            cum2 = sm_ref[rows, 0:CHUNK]
            r2t = sm_ref[rows, CHUNK:2 * CHUNK]
            ldt2t = sm_ref[rows, 2 * CHUNK:3 * CHUNK]
            dec_split = smb_ref[rows, 0:2 * CHUNK]
            dec_f = _expand(dec_split, e2f)
            dec_b = _expand(dec_split, e2b)
            b_g = bc_ref[rows, g * B_STATE:(g + 1) * B_STATE]
            c_g = bc_ref[rows, B_GN + g * B_STATE:B_GN + (g + 1) * B_STATE]
            cb = lax.dot_general(c_g, b_g, (((1,), (1,)), ((), ())), preferred_element_type=F32)
            cb_f = jnp.where(c <= r, cb, 0.0)
            cb_b = jnp.where(c >= r, cb, 0.0)
            y_g = jnp.dot(c_g, hf_g.astype(BF16), preferred_element_type=F32) * dec_f
            y_g = y_g + jnp.dot(c_g, hin_ref[lc, g], preferred_element_type=F32) * dec_b
            x_g = x_ref[rows, cols]
            y_g = y_g + dsk[:, cols] * x_g.astype(F32)
            mats, x_blocks = [], []
            for e in range(heads_per_group):
                h = g * heads_per_group + e
                hb = B_HEADS + h
                arg_f = jnp.minimum(cum2[:, h:h + 1] + r2t[h:h + 1, :], ldt2t[h:h + 1, :])
                arg_b = jnp.minimum(cum2[:, hb:hb + 1] + r2t[hb:hb + 1, :], ldt2t[hb:hb + 1, :])
                mats.append((cb_f * jnp.exp2(arg_f) + cb_b * jnp.exp2(arg_b)).astype(BF16))
                x_blocks.append(jnp.where(col_head == e, x_g, jnp.zeros_like(x_g)))
            y_g = y_g + jnp.dot(jnp.concatenate(mats, axis=1), jnp.concatenate(x_blocks, axis=0),
                                preferred_element_type=F32)
            xw = (x_g.astype(F32) * _expand(smb_ref[rows, 2 * CHUNK:4 * CHUNK], e2f)).astype(BF16)
            upd = lax.dot_general(b_g, xw, (((0,), (0,)), ((), ())), preferred_element_type=F32)
            hf_g = hf_g * dec_f[CHUNK - 1:CHUNK, :] + upd
            yz = y_g * z_ref[rows, cols].astype(F32)
            y_ref[rows, cols] = (_rms_rows(yz) * ng_ref[:, cols]).astype(BF16)
        hf_ref[g] = hf_g

    if emit_final:
        @pl.when(j == pl.num_programs(1) - 1)
        def _():
            _emit_state(hfin_ref, hf_ref)


def _ssd_fwd(p, small, small_bf, hin, e2f, e2b, d_skip, norm_g, h0, n_batch, n_chunks, emit_final, cps=2):
    m = p.shape[0]
    has_init = h0 is not None
    steps = n_chunks // cps
    rows = cps * CHUNK
    blk = lambda b, j: b * steps + j
    const = lambda b, j: (0, 0)
    in_specs = [
        pl.BlockSpec((rows, B_WIDTH), lambda b, j: (blk(b, j), P_X)),
        pl.BlockSpec((rows, 2 * B_GN), lambda b, j: (blk(b, j), P_BC)),
        pl.BlockSpec((rows, 3 * CHUNK), lambda b, j: (blk(b, j), 0)),
        pl.BlockSpec((rows, 4 * CHUNK), lambda b, j: (blk(b, j), 0)),
        pl.BlockSpec((rows, B_WIDTH), lambda b, j: (blk(b, j), P_ZB)),
        pl.BlockSpec((cps, B_GROUPS, B_STATE, B_GROUP_CH), lambda b, j: (blk(b, j), 0, 0, 0)),
        pl.BlockSpec((2 * CHUNK, B_WIDTH), const),
        pl.BlockSpec((2 * CHUNK, B_WIDTH), const),
        pl.BlockSpec((2, B_WIDTH), const),
        pl.BlockSpec((1, B_WIDTH), const),
    ]
    args = [p, p, small, small_bf, p, hin, e2f, e2b, d_skip, norm_g]
    if has_init:
        in_specs.append(pl.BlockSpec((1, B_WIDTH, B_STATE), lambda b, j: (b, 0, 0)))
        args.append(h0)
    out_shape = [jax.ShapeDtypeStruct((m, B_WIDTH), BF16)]
    out_specs = [pl.BlockSpec((rows, B_WIDTH), lambda b, j: (blk(b, j), 0))]
    if emit_final:
        out_shape.append(jax.ShapeDtypeStruct((n_batch, B_WIDTH, B_STATE), F32))
        out_specs.append(pl.BlockSpec((1, B_WIDTH, B_STATE), lambda b, j: (b, 0, 0)))
    return pl.pallas_call(
        functools.partial(_ssd_fwd_kernel, n_chunks=n_chunks, has_init=has_init, emit_final=emit_final),
        out_shape=tuple(out_shape),
        grid=(n_batch, steps),
        in_specs=in_specs,
        out_specs=tuple(out_specs),
        scratch_shapes=[pltpu.VMEM((B_GROUPS, B_STATE, B_GROUP_CH), F32)],
        compiler_params=_params(("parallel", "arbitrary")),
        name="l0_ssd_fwd",
    )(*args)


def _outproj_kernel(*refs, n_in):
    acts, ws = refs[:n_in], refs[n_in:2 * n_in]
    x_ref, mod_ref, g_ref, o_ref = refs[2 * n_in:]
    acc = jnp.dot(acts[0][...], ws[0][...], preferred_element_type=F32)
    for a_ref, w_ref in zip(acts[1:], ws[1:]):
        acc = acc + jnp.dot(a_ref[...], w_ref[...], preferred_element_type=F32)
    o_ref[...] = x_ref[...] + mod_ref[0, 2:3, :] * (_rms_rows(acc) * g_ref[...])


def _outproj(acts, w, x, mod, g, rows_per_cond, tm=512):
    m = x.shape[0]
    n_in = len(acts)
    kdim = acts[0].shape[1]
    row = _mod_row_map(rows_per_cond, tm)
    in_specs = [pl.BlockSpec((tm, kdim), lambda i: (i, 0)) for _ in acts]
    in_specs += [pl.BlockSpec((kdim, D_MODEL), functools.partial(lambda i, k: (k, 0), k=k),
                              pipeline_mode=pl.Buffered(1)) for k in range(n_in)]
    in_specs += [pl.BlockSpec((tm, D_MODEL), lambda i: (i, 0)),
                 pl.BlockSpec((1, 3, D_MODEL), lambda i: (row(i), 0, 0)),
                 pl.BlockSpec((1, D_MODEL), lambda i: (0, 0))]
    return pl.pallas_call(
        functools.partial(_outproj_kernel, n_in=n_in),
        out_shape=jax.ShapeDtypeStruct((m, D_MODEL), F32),
        grid=(m // tm,),
        in_specs=in_specs,
        out_specs=pl.BlockSpec((tm, D_MODEL), lambda i: (i, 0)),
        compiler_params=_params(("parallel",)),
        name="outproj",
    )(*acts, *([w] * n_in), x, mod, g)


Q_SCALE = C_HEAD_DIM ** -0.5 * LOG2E


def _inproj1_kernel(*refs, tn, emit_kv):
    it = iter(refs)
    x_ref, mod_ref, g_ref, w_ref, qn_ref, kn_ref = next(it), next(it), next(it), next(it), next(it), next(it)
    p_ref = next(it)
    k_ref = next(it) if emit_kv else None
    v_ref = next(it) if emit_kv else None
    h_ref = next(it)

    j = pl.program_id(1)
    nq = C_WIDTH // tn

    @pl.when(j == 0)
    def _():
        h_ref[...] = _prenorm(x_ref[...], mod_ref, g_ref)

    def sub_dot(s):
        return jnp.dot(h_ref[...], w_ref[:, s * SUB:(s + 1) * SUB], preferred_element_type=F32)

    @pl.when(j < nq)
    def _():
        for s in range(tn // SUB):
            acc = sub_dot(s)
            for hh in range(SUB // C_HEAD_DIM):
                q = _rms_rows(acc[:, hh * C_HEAD_DIM:(hh + 1) * C_HEAD_DIM]) * (qn_ref[...] * Q_SCALE)
                col = s * SUB + hh * C_HEAD_DIM
                p_ref[:, col:col + C_HEAD_DIM] = q.astype(BF16)

    @pl.when(j == nq)
    def _():
        for s in range(tn // SUB):
            acc = sub_dot(s)
            if s * SUB < C_KV_WIDTH:
                for hh in range(SUB // C_HEAD_DIM):
                    k = _rms_rows(acc[:, hh * C_HEAD_DIM:(hh + 1) * C_HEAD_DIM]) * kn_ref[...]
                    col = s * SUB + hh * C_HEAD_DIM
                    p_ref[:, col:col + C_HEAD_DIM] = k.astype(BF16)
                    if emit_kv:
                        k_ref[:, col:col + C_HEAD_DIM] = k
            else:
                p_ref[:, s * SUB:(s + 1) * SUB] = acc.astype(BF16)
                if emit_kv:
                    v_ref[:, s * SUB - C_KV_WIDTH:(s + 1) * SUB - C_KV_WIDTH] = acc

    @pl.when(j > nq)
    def _():
        for s in range(tn // SUB):
            p_ref[:, s * SUB:(s + 1) * SUB] = _silu_tanh(sub_dot(s)).astype(BF16)


def _inproj1(x, mod, g, w, q_norm, k_norm, rows_per_cond, emit_kv, tm=1024, tn=1024):
    m = x.shape[0]
    assert tn == 2 * C_KV_WIDTH
    row = _mod_row_map(rows_per_cond, tm)
    nq = C_WIDTH // tn
    nblk = L1_IN // tn
    out_col = lambda j: jnp.where(j < nq, j, jnp.where(j == nq, nblk - 1, j - 1))
    in_specs = [pl.BlockSpec((tm, D_MODEL), lambda i, j: (i, 0)),
                pl.BlockSpec((1, 3, D_MODEL), lambda i, j: (row(i), 0, 0)),
                pl.BlockSpec((1, D_MODEL), lambda i, j: (0, 0)),
                pl.BlockSpec((D_MODEL, tn), lambda i, j: (0, j)),
                pl.BlockSpec((1, C_HEAD_DIM), lambda i, j: (0, 0)),
                pl.BlockSpec((1, C_HEAD_DIM), lambda i, j: (0, 0))]
    out_shape = [jax.ShapeDtypeStruct((m, L1_IN), BF16)]
    out_specs = [pl.BlockSpec((tm, tn), lambda i, j: (i, out_col(j)))]
    if emit_kv:
        out_shape += [jax.ShapeDtypeStruct((m, C_KV_WIDTH), F32)] * 2
        out_specs += [pl.BlockSpec((tm, C_KV_WIDTH), lambda i, j: (i, 0))] * 2
    return pl.pallas_call(
        functools.partial(_inproj1_kernel, tn=tn, emit_kv=emit_kv),
        out_shape=tuple(out_shape),
        grid=(m // tm, nblk),
        in_specs=in_specs,
        out_specs=tuple(out_specs),
        scratch_shapes=[pltpu.VMEM((tm, D_MODEL), BF16)],
        compiler_params=_params(("parallel", "arbitrary")),
        name="l1_inproj",
    )(x, mod, g, w, q_norm, k_norm)


def _rope(x_bf16, cos, sin_signed):
    x = x_bf16.astype(F32)
    lane = lax.broadcasted_iota(jnp.int32, x.shape, 1)
    swapped = jnp.where((lane // 32) % 2 == 0, pltpu.roll(x, C_HEAD_DIM - 32, axis=1), pltpu.roll(x, 32, axis=1))
    return (x * cos + swapped * sin_signed).astype(BF16)


def _attn_kernel(*refs, has_ctx, use_rope):
    it = iter(refs)
    q_ref, z_ref, kn_ref, vn_ref = next(it), next(it), next(it), next(it)
    kc_ref = next(it) if has_ctx else None
    vc_ref = next(it) if has_ctx else None
    if use_rope:
        cosq_ref, sinq_ref, cosk_ref, sink_ref = next(it), next(it), next(it), next(it)
    o_ref, kk_ref, vv_ref = next(it), next(it), next(it)
    past = kc_ref.shape[1] if has_ctx else 0
    n_new = kn_ref.shape[0]
    rep = C_HEADS // C_KV_HEADS
    nt = (((1,), (1,)), ((), ()))

    @pl.when(pl.program_id(1) == 0)
    def _():
        for kv in range(C_KV_HEADS):
            kcols = slice(kv * C_HEAD_DIM, (kv + 1) * C_HEAD_DIM)
            if has_ctx:
                kk_ref[kv, 0:past, :] = kc_ref[0, :, kcols]
                vv_ref[kv, 0:past, 0:C_HEAD_DIM] = vc_ref[0, :, kcols]
            k_new = kn_ref[:, kcols]
            kk_ref[kv, past:past + n_new, :] = _rope(k_new, cosk_ref[...], sink_ref[...]) if use_rope else k_new
            vv_ref[kv, past:past + n_new, 0:C_HEAD_DIM] = vn_ref[:, kcols]
            vv_ref[kv, :, C_HEAD_DIM:2 * C_HEAD_DIM] = jnp.ones((past + n_new, C_HEAD_DIM), BF16)

    for kv in range(C_KV_HEADS):
        for e in range(rep):
            cols = slice((kv * rep + e) * C_HEAD_DIM, (kv * rep + e + 1) * C_HEAD_DIM)
            q = _rope(q_ref[:, cols], cosq_ref[...], sinq_ref[...]) if use_rope else q_ref[:, cols]
            s = lax.dot_general(q, kk_ref[kv], nt, preferred_element_type=F32)
            mx = jnp.max(s, axis=-1, keepdims=True)
            o = jnp.dot(jnp.exp2(s - mx).astype(BF16), vv_ref[kv], preferred_element_type=F32)
            out = o[:, :C_HEAD_DIM] * (1.0 / o[:, C_HEAD_DIM:]) * z_ref[:, cols].astype(F32)
            o_ref[:, cols] = out.astype(BF16)


def _attention(qkvz, seq_len, n_batch, tq, ctx_k=None, ctx_v=None, rope=None):
    m = qkvz.shape[0]
    qb = seq_len // tq
    kcol = (2 * C_WIDTH) // C_KV_WIDTH
    has_ctx = ctx_k is not None
    use_rope = rope is not None
    past = ctx_k.shape[1] if has_ctx else 0
    in_specs = [pl.BlockSpec((tq, C_WIDTH), lambda b, i: (b * qb + i, 0)),
                pl.BlockSpec((tq, C_WIDTH), lambda b, i: (b * qb + i, 1)),
                pl.BlockSpec((seq_len, C_KV_WIDTH), lambda b, i: (b, kcol)),
                pl.BlockSpec((seq_len, C_KV_WIDTH), lambda b, i: (b, kcol + 1))]
    args = [qkvz, qkvz, qkvz, qkvz]
    if has_ctx:
        in_specs += [pl.BlockSpec((1, past, C_KV_WIDTH), lambda b, i: (b, 0, 0))] * 2
        args += [ctx_k, ctx_v]
    if use_rope:
        in_specs += [pl.BlockSpec((tq, C_HEAD_DIM), lambda b, i: (i, 0))] * 2
        in_specs += [pl.BlockSpec((seq_len, C_HEAD_DIM), lambda b, i: (0, 0))] * 2
        args += [*rope, *rope]
    return pl.pallas_call(
        functools.partial(_attn_kernel, has_ctx=has_ctx, use_rope=use_rope),
        out_shape=jax.ShapeDtypeStruct((m, C_WIDTH), BF16),
        grid=(n_batch, qb),
        in_specs=in_specs,
        out_specs=pl.BlockSpec((tq, C_WIDTH), lambda b, i: (b * qb + i, 0)),
        scratch_shapes=[pltpu.VMEM((C_KV_HEADS, past + seq_len, C_HEAD_DIM), BF16),
                        pltpu.VMEM((C_KV_HEADS, past + seq_len, 2 * C_HEAD_DIM), BF16)],
        compiler_params=_params(("parallel", "arbitrary")),
        name="l1_attention",
    )(*args)


def _expand_matrix(first_row):
    e = np.zeros((2 * CHUNK, B_WIDTH), np.float32)
    ch = np.arange(B_WIDTH)
    e[first_row + ch // B_HEAD_DIM, ch] = 1.0
    e[CHUNK + first_row + ch // B_HEAD_DIM, ch] = 1.0
    return jnp.asarray(e, BF16)


def _rope_tables(n):
    t = np.arange(n)
    pos = np.stack([t // GRID_W, t % GRID_W], axis=1).astype(np.float64)
    half = C_HEAD_DIM // 2
    inv = ROPE_THETA ** (-np.arange(0, half, 2, dtype=np.float64) / half)
    ang = pos[:, :, None] * inv[None, None, :]
    cos = np.concatenate([np.cos(ang), np.cos(ang)], axis=-1).reshape(n, C_HEAD_DIM)
    sin = np.concatenate([-np.sin(ang), np.sin(ang)], axis=-1).reshape(n, C_HEAD_DIM)
    return jnp.asarray(cos, F32), jnp.asarray(sin, F32)


def _pad_lanes(x, width=128):
    return jnp.pad(x, ((0, 0), (0, width - x.shape[1])))


def kernel(x_prompt, x_sample, state_l0_ssm_fwd, state_l0_ssm_bwd, cache_l1_k, cache_l1_v, c, c_ctx, mod_w0, mod_b0, norm_pre0, norm_post0, l0_w_in, l0_v_gain, l0_w_s, l0_b_s, l0_conv_w, l0_conv_b, l0_dt_bias, l0_a_log, l0_d_skip, l0_ssm_norm, l0_w_out, mod_w1, mod_b1, norm_pre1, norm_post1, l1_w_in, l1_q_norm, l1_k_norm, l1_w_out):
    pb, pn, d = x_prompt.shape
    sb, sn, _ = x_sample.shape
    xp = x_prompt.reshape(pb * pn, d)
    xs = x_sample.reshape(sb * sn, d)
    row = lambda v: v.reshape(1, -1)

    cond = jnp.zeros((MOD_ROWS, d), F32).at[:sb].set(c).at[CTX_ROW].set(c_ctx)
    w0_main = l0_w_in.astype(BF16)
    w0_dt = _pad_lanes(l0_w_in[:, L0_MAIN:]).astype(BF16)
    w0_out = l0_w_out.astype(BF16)
    ws = l0_w_s.astype(BF16)
    bs = jnp.broadcast_to(l0_b_s[:, :, None], (A_GROUPS, CHUNK, CHUNK))
    conv_w = jnp.pad(l0_conv_w, ((0, 8 - B_CONV), (0, 0)))
    dt_bias = _pad_lanes(l0_dt_bias.reshape(1, 2 * B_HEADS))
    a_log = _pad_lanes(l0_a_log.reshape(1, 2 * B_HEADS))
    d_skip = jnp.repeat(l0_d_skip, B_HEAD_DIM, axis=1)
    e2f, e2b = _expand_matrix(0), _expand_matrix(B_HEADS)
    w1_in = l1_w_in.astype(BF16)
    w1_out = l1_w_out.astype(BF16)
    ctx_k = cache_l1_k.reshape(sb, -1, C_KV_WIDTH).astype(BF16)
    ctx_v = cache_l1_v.reshape(sb, -1, C_KV_WIDTH).astype(BF16)
    rope = _rope_tables(sn)

    mod0 = _modulation(cond, mod_w0, mod_b0)
    mod1 = _modulation(cond, mod_w1, mod_b1)

    def layer0(x, rows_per_cond, n_batch, seq, h0f, h0b, emit_final):
        n_chunks = seq // CHUNK
        p, dt_raw = _inproj0(x, mod0, row(norm_pre0), w0_main, w0_dt, conv_w, row(l0_conv_b),
                             rows_per_cond, seq)
        a_out = _mixer_a(p, row(l0_v_gain), ws, bs)
        bwd = _ssd_bwd(p, dt_raw, dt_bias, a_log, e2b, h0b, n_batch, n_chunks, emit_final,
                       cps=min(4, n_chunks))
        fwd = _ssd_fwd(p, bwd[1], bwd[2], bwd[0], e2f, e2b, d_skip, row(l0_ssm_norm),
                       h0f, n_batch, n_chunks, emit_final)
        x1 = _outproj([a_out, fwd[0]], w0_out, x, mod0, row(norm_post0), rows_per_cond)
        return x1, (fwd[1] if emit_final else None), (bwd[3] if emit_final else None)

    st_shape = (sb, B_WIDTH, B_STATE)
    xp1, new_f, new_b = layer0(xp, None, pb, pn, None, None, True)
    xs1, _, _ = layer0(xs, sn, sb, sn, state_l0_ssm_fwd.reshape(st_shape),
                       state_l0_ssm_bwd.reshape(st_shape), False)

    qp, new_k, new_v = _inproj1(xp1, mod1, row(norm_pre1), w1_in, row(l1_q_norm), row(l1_k_norm),
                                None, True)
    op = _attention(qp, pn, pb, pn)
    yp = _outproj([op], w1_out, xp1, mod1, row(norm_post1), None)

    (qs,) = _inproj1(xs1, mod1, row(norm_pre1), w1_in, row(l1_q_norm), row(l1_k_norm), sn, False)
    os_ = _attention(qs, sn, sb, min(sn, 512), ctx_k, ctx_v, rope)
    ys = _outproj([os_], w1_out, xs1, mod1, row(norm_post1), sn)

    return (yp.reshape(pb, pn, d), ys.reshape(sb, sn, d),
            new_f.reshape(pb, B_HEADS, B_HEAD_DIM, B_STATE),
            new_b.reshape(pb, B_HEADS, B_HEAD_DIM, B_STATE),
            new_k.reshape(pb, pn, C_KV_HEADS, C_HEAD_DIM),
            new_v.reshape(pb, pn, C_KV_HEADS, C_HEAD_DIM))
```

```python
import functools

import numpy as np
import jax
import jax.numpy as jnp
from jax import lax
from jax.experimental import pallas as pl
from jax.experimental.pallas import tpu as pltpu

F32 = jnp.float32
BF16 = jnp.bfloat16

EPS = 1e-6
LOG2E = 1.4426950408889634
GELU_K = 0.7978845608028654

D_MODEL = 2048
CHUNK = 128
GRID_W = 64
ROPE_THETA = 10000.0
A_WIDTH = 2048
A_GROUPS = 16
B_WIDTH = 2048
B_HEAD_DIM = 64
B_HEADS = 32
B_GROUPS = 8
B_STATE = 128
B_CONV = 5
B_GN = B_GROUPS * B_STATE
B_CONV_CH = B_WIDTH + 2 * B_GN
B_GROUP_CH = B_WIDTH // B_GROUPS
L0_MAIN = 2 * A_WIDTH + A_WIDTH + B_WIDTH + B_CONV_CH
C_HEADS = 16
C_KV_HEADS = 4
C_HEAD_DIM = 128
C_WIDTH = 2048
C_KV_WIDTH = 512
L1_IN = C_WIDTH + 2 * C_KV_WIDTH + C_WIDTH

MOD_ROWS = 16
CTX_ROW = 8
V7X_VMEM_LIMIT = 56 * 1024 * 1024
SUB = 256
GAP = 16


def _silu(x):
    return x * (1.0 / (1.0 + jnp.exp(-x)))


def _silu_tanh(x):
    return x * (0.5 + 0.5 * jnp.tanh(0.5 * x))


def _softplus(x):
    return jnp.maximum(x, 0.0) + jnp.log1p(jnp.exp(-jnp.abs(x)))


def _rms_rows(x):
    return x * lax.rsqrt(jnp.mean(x * x, axis=-1, keepdims=True) + EPS)


def _params(sem, vmem=V7X_VMEM_LIMIT):
    return pltpu.CompilerParams(dimension_semantics=sem, vmem_limit_bytes=vmem)


def _mod_kernel(c_ref, w_ref, b_ref, o_ref):
    s = _silu(c_ref[...]).astype(BF16)
    o_ref[...] = jnp.dot(s, w_ref[...].astype(BF16), preferred_element_type=F32) + b_ref[...]


def _modulation(cond, w, b, tn=512):
    n = w.shape[1]
    m = pl.pallas_call(
        _mod_kernel,
        out_shape=jax.ShapeDtypeStruct((MOD_ROWS, n), F32),
        grid=(n // tn,),
        in_specs=[pl.BlockSpec((MOD_ROWS, D_MODEL), lambda j: (0, 0)),
                  pl.BlockSpec((D_MODEL, tn), lambda j: (0, j)),
                  pl.BlockSpec((1, tn), lambda j: (0, j))],
        out_specs=pl.BlockSpec((MOD_ROWS, tn), lambda j: (0, j)),
        compiler_params=_params(("arbitrary",)),
        name="modulation",
    )(cond, w, b.reshape(1, n))
    return m.reshape(MOD_ROWS, 3, D_MODEL)


def _mod_row_map(rows_per_cond, tm):
    if rows_per_cond is None:
        return lambda i: CTX_ROW
    return lambda i: (i * tm) // rows_per_cond


def _prenorm(x, mod_ref, g_ref):
    gain = g_ref[...] * (1.0 + mod_ref[0, 1:2, :])
    r = lax.rsqrt(jnp.mean(x * x, axis=-1, keepdims=True) + EPS)
    return (x * r * gain + mod_ref[0, 0:1, :]).astype(BF16)


def _inproj0_kernel(xp_ref, x_ref, xn_ref, mod_ref, g_ref, w_ref, wdt_ref, cw_ref, cb_ref,
                    p_ref, dt_ref, h_ref, hg_ref, acc_ref, *, tm, tn, seq, n_gelu, n_gated):
    i = pl.program_id(0)
    j = pl.program_id(1)
    nseg = max(1, tm // seq)
    seg = tm // nseg
    rows_g = hg_ref.shape[0]

    @pl.when(j == 0)
    def _():
        h = _prenorm(x_ref[...], mod_ref, g_ref)
        h_ref[...] = h
        dt_ref[...] = jnp.dot(h, wdt_ref[...], preferred_element_type=F32)
        zero = jnp.zeros((GAP, D_MODEL), BF16)
        has_prev = (i * tm) % seq != 0
        has_next = ((i + 1) * tm) % seq != 0
        hg_ref[0:GAP, :] = jnp.where(has_prev, _prenorm(xp_ref[...], mod_ref, g_ref), zero)
        for q in range(nseg):
            base = GAP + q * (seg + GAP)
            hg_ref[base:base + seg, :] = h[q * seg:(q + 1) * seg]
            if q < nseg - 1:
                hg_ref[base + seg:base + seg + GAP, :] = zero
        hg_ref[rows_g - GAP:rows_g, :] = jnp.where(has_next, _prenorm(xn_ref[...], mod_ref, g_ref), zero)

    @pl.when(j < n_gelu)
    def _():
        for s in range(tn // SUB):
            sub = slice(s * SUB, (s + 1) * SUB)
            acc = jnp.dot(h_ref[...], w_ref[:, sub], preferred_element_type=F32)
            t = jnp.tanh(acc * (GELU_K + (GELU_K * 0.044715) * (acc * acc)))
            p_ref[:, sub] = (acc * (0.5 + 0.5 * t)).astype(BF16)

    @pl.when((j >= n_gelu) & (j < n_gated))
    def _():
        for s in range(tn // SUB):
            sub = slice(s * SUB, (s + 1) * SUB)
            acc = jnp.dot(h_ref[...], w_ref[:, sub], preferred_element_type=F32)
            p_ref[:, sub] = _silu_tanh(acc).astype(BF16)

    @pl.when(j >= n_gated)
    def _():
        for s in range(tn // SUB):
            sub = slice(s * SUB, (s + 1) * SUB)
            buf = s % 2
            acc_ref[buf] = jnp.dot(hg_ref[...], w_ref[:, sub], preferred_element_type=F32)
            for q in range(nseg):
                base = GAP + q * (seg + GAP)
                out = None
                for k in range(B_CONV):
                    tap = acc_ref[buf, pl.ds(base - B_CONV // 2 + k, seg), :]
                    term = tap * cw_ref[k:k + 1, sub]
                    out = term if out is None else out + term
                p_ref[q * seg:(q + 1) * seg, sub] = _silu_tanh(out + cb_ref[:, sub]).astype(BF16)


def _inproj0(x, mod, g, w_main, w_dt, conv_w, conv_b, rows_per_cond, seq, tm=1024, tn=1024):
    m = x.shape[0]
    row = _mod_row_map(rows_per_cond, tm)
    nseg = max(1, tm // seq)
    rows_g = tm + (nseg + 1) * GAP
    n_gated = (L0_MAIN - B_CONV_CH) // tn
    gb = tm // GAP
    last_gap = m // GAP - 1
    conv_col = lambda j: jnp.maximum(j - n_gated, 0)
    kern = functools.partial(_inproj0_kernel, tm=tm, tn=tn, seq=seq, n_gelu=2 * A_WIDTH // tn, n_gated=n_gated)
    return pl.pallas_call(
        kern,
        out_shape=(jax.ShapeDtypeStruct((m, L0_MAIN), BF16),
                   jax.ShapeDtypeStruct((m, 128), F32)),
        grid=(m // tm, L0_MAIN // tn),
        in_specs=[pl.BlockSpec((GAP, D_MODEL), lambda i, j: (jnp.maximum(i * gb - 1, 0), 0)),
                  pl.BlockSpec((tm, D_MODEL), lambda i, j: (i, 0)),
                  pl.BlockSpec((GAP, D_MODEL), lambda i, j: (jnp.minimum((i + 1) * gb, last_gap), 0)),
                  pl.BlockSpec((1, 3, D_MODEL), lambda i, j: (row(i), 0, 0)),
                  pl.BlockSpec((1, D_MODEL), lambda i, j: (0, 0)),
                  pl.BlockSpec((D_MODEL, tn), lambda i, j: (0, j)),
                  pl.BlockSpec((D_MODEL, 128), lambda i, j: (0, 0)),
                  pl.BlockSpec((8, tn), lambda i, j: (0, conv_col(j))),
                  pl.BlockSpec((1, tn), lambda i, j: (0, conv_col(j)))],
        out_specs=(pl.BlockSpec((tm, tn), lambda i, j: (i, j)),
                   pl.BlockSpec((tm, 128), lambda i, j: (i, 0))),
        scratch_shapes=[pltpu.VMEM((tm, D_MODEL), BF16),
                        pltpu.VMEM((rows_g, D_MODEL), BF16),
                        pltpu.VMEM((2, rows_g, SUB), F32)],
        compiler_params=_params(("parallel", "arbitrary")),
        name="l0_inproj",
    )(x, x, x, mod, g, w_main, w_dt, conv_w, conv_b)


P_U, P_V, P_ZA, P_ZB, P_X = 0, 1, 2, 3, 4
P_BC = 5
P_B = (L0_MAIN - 2 * B_GN) // B_GN


def _mixa_kernel(u_ref, v_ref, z_ref, vg_ref, ws_ref, bs_ref, o_ref, *, n_chunks):
    v = v_ref[...].astype(F32)
    vc = v - jnp.mean(v, axis=-1, keepdims=True)
    vn = vc * lax.rsqrt(jnp.mean(vc * vc, axis=-1, keepdims=True) + EPS) * vg_ref[...]
    vn = vn.astype(BF16)
    gd = A_WIDTH // A_GROUPS
    for c in range(n_chunks):
        rows = slice(c * CHUNK, (c + 1) * CHUNK)
        for g in range(A_GROUPS):
            cols = slice(g * gd, (g + 1) * gd)
            s = jnp.dot(ws_ref[g], vn[rows, cols], preferred_element_type=F32) + bs_ref[g]
            o = u_ref[rows, cols].astype(F32) * s * z_ref[rows, cols].astype(F32)
            o_ref[rows, cols] = o.astype(BF16)


def _split2(x):
    hi = x.astype(BF16)
    lo = (x - hi.astype(F32)).astype(BF16)
    return jnp.concatenate([hi, lo], axis=1)


def _split3(x):
    hi = x.astype(BF16)
    r = x - hi.astype(F32)
    mid = r.astype(BF16)
    lo = (r - mid.astype(F32)).astype(BF16)
    return hi, mid, lo


def _tri_matmul(tri, x):
    hi, mid, lo = _split3(x)
    out = jnp.dot(tri, lo, preferred_element_type=F32)
    out = out + jnp.dot(tri, mid, preferred_element_type=F32)
    return out + jnp.dot(tri, hi, preferred_element_type=F32)


def _expand(x_split, e2):
    return jnp.dot(x_split, e2, preferred_element_type=F32)


def _ssd_small(dt_raw, dtb_ref, alog_ref):
    dt = _softplus(dt_raw + dtb_ref[...])
    da = dt * (-jnp.exp(alog_ref[...]))
    r = lax.broadcasted_iota(jnp.int32, (CHUNK, CHUNK), 0)
    c = lax.broadcasted_iota(jnp.int32, (CHUNK, CHUNK), 1)
    lower = jnp.where(c <= r, 1.0, 0.0).astype(BF16)
    upper = jnp.where(c >= r, 1.0, 0.0).astype(BF16)
    cum = jnp.where(c < B_HEADS, _tri_matmul(lower, da), _tri_matmul(upper, da))
    return dt, cum, r, c


def _init_state(state_ref, h0_ref):
    for g in range(B_GROUPS):
        if h0_ref is not None:
            state_ref[g] = h0_ref[0, g * B_GROUP_CH:(g + 1) * B_GROUP_CH, :].T
        else:
            state_ref[g] = jnp.zeros((B_STATE, B_GROUP_CH), F32)


def _emit_state(hfin_ref, state_ref):
    for g in range(B_GROUPS):
        hfin_ref[0, g * B_GROUP_CH:(g + 1) * B_GROUP_CH, :] = state_ref[g].T


def _ssd_bwd_kernel(*refs, n_chunks, has_init, emit_final):
    it = iter(refs)
    x_ref, b_ref, dtraw_ref, dtb_ref, alog_ref, e2b_ref = (next(it), next(it), next(it), next(it),
                                                          next(it), next(it))
    mixa_in = [next(it) for _ in range(6)]
    h0_ref = next(it) if has_init else None
    hin_ref, sm_ref, smb_ref, a_ref = next(it), next(it), next(it), next(it)
    hfin_ref = next(it) if emit_final else None
    hb_ref = next(it)

    j = pl.program_id(1)
    cps = hin_ref.shape[0]

    @pl.when(j == 0)
    def _():
        _init_state(hb_ref, h0_ref)

    _mixa_kernel(*mixa_in, a_ref, n_chunks=cps)

    xws, tots = [], []
    for lc in range(cps):
        rows = slice(lc * CHUNK, (lc + 1) * CHUNK)
        dt, cum, _, lane = _ssd_small(dtraw_ref[rows, :], dtb_ref, alog_ref)
        is_bwd = (lane >= B_HEADS) & (lane < 2 * B_HEADS)
        w_b = jnp.where(is_bwd, dt * jnp.exp(jnp.where(is_bwd, cum[0:1, :] - cum, 0.0)), 0.0)
        xws.append((x_ref[rows, :].astype(F32) * _expand(_split2(w_b), e2b_ref[...])).astype(BF16))
        tot_split = _split2(jnp.broadcast_to(jnp.exp(cum[0:1, :]), (GAP, CHUNK)))
        tots.append(_expand(tot_split, e2b_ref[...])[0:1, :])
        cum2 = cum * LOG2E
        ldt2 = jnp.log(dt) * LOG2E
        sm_ref[rows, 0:CHUNK] = cum2
        sm_ref[rows, CHUNK:2 * CHUNK] = (ldt2 - cum2).T
        sm_ref[rows, 2 * CHUNK:3 * CHUNK] = ldt2.T
        is_fwd = lane < B_HEADS
        to_end = jnp.exp(jnp.where(is_fwd, cum[CHUNK - 1:CHUNK, :] - cum, 0.0))
        smb_ref[rows, 0:2 * CHUNK] = _split2(jnp.exp(cum))
        smb_ref[rows, 2 * CHUNK:4 * CHUNK] = _split2(jnp.where(is_fwd, dt * to_end, 0.0))

    for g in range(B_GROUPS):
        cols = slice(g * B_GROUP_CH, (g + 1) * B_GROUP_CH)
        hb_g = hb_ref[g]
        for lc in reversed(range(cps)):
            rows = slice(lc * CHUNK, (lc + 1) * CHUNK)
            hin_ref[lc, g] = hb_g.astype(BF16)
            upd = lax.dot_general(b_ref[rows, g * B_STATE:(g + 1) * B_STATE], xws[lc][:, cols],
                                  (((0,), (0,)), ((), ())), preferred_element_type=F32)
            hb_g = hb_g * tots[lc][:, cols] + upd
        hb_ref[g] = hb_g

    if emit_final:
        @pl.when(j == pl.num_programs(1) - 1)
        def _():
            _emit_state(hfin_ref, hb_ref)


def _ssd_bwd(p, dt_raw, dt_bias, a_log, e2b, v_gain, w_s, b_s, h0, n_batch, n_chunks, emit_final, cps=2):
    has_init = h0 is not None
    steps = n_chunks // cps
    rows = cps * CHUNK
    blk = lambda b, j: b * steps + (steps - 1 - j)
    const = lambda b, j: (0, 0)
    in_specs = [
        pl.BlockSpec((rows, B_WIDTH), lambda b, j: (blk(b, j), P_X)),
        pl.BlockSpec((rows, B_GN), lambda b, j: (blk(b, j), P_B)),
        pl.BlockSpec((rows, 128), lambda b, j: (blk(b, j), 0)),
        pl.BlockSpec((1, 128), const),
        pl.BlockSpec((1, 128), const),
        pl.BlockSpec((2 * CHUNK, B_WIDTH), const),
        pl.BlockSpec((rows, A_WIDTH), lambda b, j: (blk(b, j), P_U)),
        pl.BlockSpec((rows, A_WIDTH), lambda b, j: (blk(b, j), P_V)),
        pl.BlockSpec((rows, A_WIDTH), lambda b, j: (blk(b, j), P_ZA)),
        pl.BlockSpec((1, A_WIDTH), const),
        pl.BlockSpec((A_GROUPS, CHUNK, CHUNK), lambda b, j: (0, 0, 0)),
        pl.BlockSpec((A_GROUPS, CHUNK, CHUNK), lambda b, j: (0, 0, 0)),
    ]
    args = [p, p, dt_raw, dt_bias, a_log, e2b, p, p, p, v_gain, w_s, b_s]
    if has_init:
        in_specs.append(pl.BlockSpec((1, B_WIDTH, B_STATE), lambda b, j: (b, 0, 0)))
        args.append(h0)
    m = n_batch * n_chunks * CHUNK
    out_shape = [jax.ShapeDtypeStruct((n_batch * n_chunks, B_GROUPS, B_STATE, B_GROUP_CH), BF16),
                 jax.ShapeDtypeStruct((m, 3 * CHUNK), F32),
                 jax.ShapeDtypeStruct((m, 4 * CHUNK), BF16),
                 jax.ShapeDtypeStruct((m, A_WIDTH), BF16)]
    out_specs = [pl.BlockSpec((cps, B_GROUPS, B_STATE, B_GROUP_CH), lambda b, j: (blk(b, j), 0, 0, 0)),
                 pl.BlockSpec((rows, 3 * CHUNK), lambda b, j: (blk(b, j), 0)),
                 pl.BlockSpec((rows, 4 * CHUNK), lambda b, j: (blk(b, j), 0)),
                 pl.BlockSpec((rows, A_WIDTH), lambda b, j: (blk(b, j), 0))]
    if emit_final:
        out_shape.append(jax.ShapeDtypeStruct((n_batch, B_WIDTH, B_STATE), F32))
        out_specs.append(pl.BlockSpec((1, B_WIDTH, B_STATE), lambda b, j: (b, 0, 0)))
    return pl.pallas_call(
        functools.partial(_ssd_bwd_kernel, n_chunks=n_chunks, has_init=has_init, emit_final=emit_final),
        out_shape=tuple(out_shape),
        grid=(n_batch, steps),
        in_specs=in_specs,
        out_specs=tuple(out_specs),
        scratch_shapes=[pltpu.VMEM((B_GROUPS, B_STATE, B_GROUP_CH), F32)],
        compiler_params=_params(("parallel", "arbitrary")),
        name="l0_ssd_bwd",
    )(*args)


def _ssd_fwd_kernel(*refs, n_chunks, has_init, emit_final):
    it = iter(refs)
    x_ref, bc_ref, sm_ref, smb_ref, z_ref, hin_ref = (next(it), next(it), next(it), next(it),
                                                      next(it), next(it))
    e2f_ref, e2b_ref, dsk_ref, ng_ref = next(it), next(it), next(it), next(it)
    h0_ref = next(it) if has_init else None
    y_ref = next(it)
    hfin_ref = next(it) if emit_final else None
    hf_ref = next(it)

    j = pl.program_id(1)

    @pl.when(j == 0)
    def _():
        _init_state(hf_ref, h0_ref)

    cps = hin_ref.shape[0]
    r = lax.broadcasted_iota(jnp.int32, (CHUNK, CHUNK), 0)
    c = lax.broadcasted_iota(jnp.int32, (CHUNK, CHUNK), 1)
    dsk = dsk_ref[0:1, :] + dsk_ref[1:2, :]
    heads_per_group = B_HEADS // B_GROUPS
    col_head = lax.broadcasted_iota(jnp.int32, (CHUNK, B_GROUP_CH), 1) // B_HEAD_DIM

    for g in range(B_GROUPS):
        cols = slice(g * B_GROUP_CH, (g + 1) * B_GROUP_CH)
        e2f = e2f_ref[:, cols]
        e2b = e2b_ref[:, cols]
        hf_g = hf_ref[g]
        for lc in range(cps):
            rows = slice(lc * CHUNK, (lc + 1) * CHUNK)
            cum2 = sm_ref[rows, 0:CHUNK]
            r2t = sm_ref[rows, CHUNK:2 * CHUNK]
            ldt2t = sm_ref[rows, 2 * CHUNK:3 * CHUNK]
            dec_split = smb_ref[rows, 0:2 * CHUNK]
            dec_f = _expand(dec_split, e2f)
            dec_b = _expand(dec_split, e2b)
            b_g = bc_ref[rows, g * B_STATE:(g + 1) * B_STATE]
            c_g = bc_ref[rows, B_GN + g * B_STATE:B_GN + (g + 1) * B_STATE]
            cb = lax.dot_general(c_g, b_g, (((1,), (1,)), ((), ())), preferred_element_type=F32)
            cb_f = jnp.where(c <= r, cb, 0.0)
            cb_b = jnp.where(c >= r, cb, 0.0)
            y_g = jnp.dot(c_g, hf_g.astype(BF16), preferred_element_type=F32) * dec_f
            y_g = y_g + jnp.dot(c_g, hin_ref[lc, g], preferred_element_type=F32) * dec_b
            x_g = x_ref[rows, cols]
            y_g = y_g + dsk[:, cols] * x_g.astype(F32)
            mats, x_blocks = [], []
            for e in range(heads_per_group):
                h = g * heads_per_group + e
                hb = B_HEADS + h
                arg_f = jnp.minimum(cum2[:, h:h + 1] + r2t[h:h + 1, :], ldt2t[h:h + 1, :])
                arg_b = jnp.minimum(cum2[:, hb:hb + 1] + r2t[hb:hb + 1, :], ldt2t[hb:hb + 1, :])
                mats.append((cb_f * jnp.exp2(arg_f) + cb_b * jnp.exp2(arg_b)).astype(BF16))
                x_blocks.append(jnp.where(col_head == e, x_g, jnp.zeros_like(x_g)))
            y_g = y_g + jnp.dot(jnp.concatenate(mats, axis=1), jnp.concatenate(x_blocks, axis=0),
                                preferred_element_type=F32)
            xw = (x_g.astype(F32) * _expand(smb_ref[rows, 2 * CHUNK:4 * CHUNK], e2f)).astype(BF16)
            upd = lax.dot_general(b_g, xw, (((0,), (0,)), ((), ())), preferred_element_type=F32)
            hf_g = hf_g * dec_f[CHUNK - 1:CHUNK, :] + upd
            yz = y_g * z_ref[rows, cols].astype(F32)
            y_ref[rows, cols] = (_rms_rows(yz) * ng_ref[:, cols]).astype(BF16)
        hf_ref[g] = hf_g

    if emit_final:
        @pl.when(j == pl.num_programs(1) - 1)
        def _():
            _emit_state(hfin_ref, hf_ref)


def _ssd_fwd(p, small, small_bf, hin, e2f, e2b, d_skip, norm_g, h0, n_batch, n_chunks, emit_final, cps=2):
    m = p.shape[0]
    has_init = h0 is not None
    steps = n_chunks // cps
    rows = cps * CHUNK
    blk = lambda b, j: b * steps + j
    const = lambda b, j: (0, 0)
    in_specs = [
        pl.BlockSpec((rows, B_WIDTH), lambda b, j: (blk(b, j), P_X)),
        pl.BlockSpec((rows, 2 * B_GN), lambda b, j: (blk(b, j), P_BC)),
        pl.BlockSpec((rows, 3 * CHUNK), lambda b, j: (blk(b, j), 0)),
        pl.BlockSpec((rows, 4 * CHUNK), lambda b, j: (blk(b, j), 0)),
        pl.BlockSpec((rows, B_WIDTH), lambda b, j: (blk(b, j), P_ZB)),
        pl.BlockSpec((cps, B_GROUPS, B_STATE, B_GROUP_CH), lambda b, j: (blk(b, j), 0, 0, 0)),
        pl.BlockSpec((2 * CHUNK, B_WIDTH), const),
        pl.BlockSpec((2 * CHUNK, B_WIDTH), const),
        pl.BlockSpec((2, B_WIDTH), const),
        pl.BlockSpec((1, B_WIDTH), const),
    ]
    args = [p, p, small, small_bf, p, hin, e2f, e2b, d_skip, norm_g]
    if has_init:
        in_specs.append(pl.BlockSpec((1, B_WIDTH, B_STATE), lambda b, j: (b, 0, 0)))
        args.append(h0)
    out_shape = [jax.ShapeDtypeStruct((m, B_WIDTH), BF16)]
    out_specs = [pl.BlockSpec((rows, B_WIDTH), lambda b, j: (blk(b, j), 0))]
    if emit_final:
        out_shape.append(jax.ShapeDtypeStruct((n_batch, B_WIDTH, B_STATE), F32))
        out_specs.append(pl.BlockSpec((1, B_WIDTH, B_STATE), lambda b, j: (b, 0, 0)))
    return pl.pallas_call(
        functools.partial(_ssd_fwd_kernel, n_chunks=n_chunks, has_init=has_init, emit_final=emit_final),
        out_shape=tuple(out_shape),
        grid=(n_batch, steps),
        in_specs=in_specs,
        out_specs=tuple(out_specs),
        scratch_shapes=[pltpu.VMEM((B_GROUPS, B_STATE, B_GROUP_CH), F32)],
        compiler_params=_params(("parallel", "arbitrary")),
        name="l0_ssd_fwd",
    )(*args)


def _outproj_kernel(*refs, n_in):
    acts, ws = refs[:n_in], refs[n_in:2 * n_in]
    x_ref, mod_ref, g_ref, o_ref = refs[2 * n_in:]
    acc = jnp.dot(acts[0][...], ws[0][...], preferred_element_type=F32)
    for a_ref, w_ref in zip(acts[1:], ws[1:]):
        acc = acc + jnp.dot(a_ref[...], w_ref[...], preferred_element_type=F32)
    gain = mod_ref[0, 2:3, :] * g_ref[...]
    r = lax.rsqrt(jnp.mean(acc * acc, axis=-1, keepdims=True) + EPS)
    o_ref[...] = x_ref[...] + acc * r * gain


def _outproj(acts, w, x, mod, g, rows_per_cond, tm=512):
    m = x.shape[0]
    n_in = len(acts)
    kdim = acts[0].shape[1]
    row = _mod_row_map(rows_per_cond, tm)
    in_specs = [pl.BlockSpec((tm, kdim), lambda i: (i, 0)) for _ in acts]
    in_specs += [pl.BlockSpec((kdim, D_MODEL), functools.partial(lambda i, k: (k, 0), k=k),
                              pipeline_mode=pl.Buffered(1)) for k in range(n_in)]
    in_specs += [pl.BlockSpec((tm, D_MODEL), lambda i: (i, 0)),
                 pl.BlockSpec((1, 3, D_MODEL), lambda i: (row(i), 0, 0)),
                 pl.BlockSpec((1, D_MODEL), lambda i: (0, 0))]
    return pl.pallas_call(
        functools.partial(_outproj_kernel, n_in=n_in),
        out_shape=jax.ShapeDtypeStruct((m, D_MODEL), F32),
        grid=(m // tm,),
        in_specs=in_specs,
        out_specs=pl.BlockSpec((tm, D_MODEL), lambda i: (i, 0)),
        compiler_params=_params(("parallel",)),
        name="outproj",
    )(*acts, *([w] * n_in), x, mod, g)


Q_SCALE = C_HEAD_DIM ** -0.5 * LOG2E


def _inproj1_kernel(*refs, tn, emit_kv):
    it = iter(refs)
    x_ref, mod_ref, g_ref, w_ref, qn_ref, kn_ref = next(it), next(it), next(it), next(it), next(it), next(it)
    p_ref = next(it)
    k_ref = next(it) if emit_kv else None
    v_ref = next(it) if emit_kv else None
    h_ref = next(it)

    j = pl.program_id(1)
    nq = C_WIDTH // tn

    @pl.when(j == 0)
    def _():
        h_ref[...] = _prenorm(x_ref[...], mod_ref, g_ref)

    def sub_dot(s):
        return jnp.dot(h_ref[...], w_ref[:, s * SUB:(s + 1) * SUB], preferred_element_type=F32)

    @pl.when(j < nq)
    def _():
        for s in range(tn // SUB):
            acc = sub_dot(s)
            for hh in range(SUB // C_HEAD_DIM):
                q = _rms_rows(acc[:, hh * C_HEAD_DIM:(hh + 1) * C_HEAD_DIM]) * (qn_ref[...] * Q_SCALE)
                col = s * SUB + hh * C_HEAD_DIM
                p_ref[:, col:col + C_HEAD_DIM] = q.astype(BF16)

    @pl.when(j == nq)
    def _():
        for s in range(tn // SUB):
            acc = sub_dot(s)
            if s * SUB < C_KV_WIDTH:
                for hh in range(SUB // C_HEAD_DIM):
                    k = _rms_rows(acc[:, hh * C_HEAD_DIM:(hh + 1) * C_HEAD_DIM]) * kn_ref[...]
                    col = s * SUB + hh * C_HEAD_DIM
                    p_ref[:, col:col + C_HEAD_DIM] = k.astype(BF16)
                    if emit_kv:
                        k_ref[:, col:col + C_HEAD_DIM] = k
            else:
                p_ref[:, s * SUB:(s + 1) * SUB] = acc.astype(BF16)
                if emit_kv:
                    v_ref[:, s * SUB - C_KV_WIDTH:(s + 1) * SUB - C_KV_WIDTH] = acc

    @pl.when(j > nq)
    def _():
        for s in range(tn // SUB):
            p_ref[:, s * SUB:(s + 1) * SUB] = _silu_tanh(sub_dot(s)).astype(BF16)


def _inproj1(x, mod, g, w, q_norm, k_norm, rows_per_cond, emit_kv, tm=1024, tn=1024):
    m = x.shape[0]
    assert tn == 2 * C_KV_WIDTH
    row = _mod_row_map(rows_per_cond, tm)
    nq = C_WIDTH // tn
    nblk = L1_IN // tn
    out_col = lambda j: jnp.where(j < nq, j, jnp.where(j == nq, nblk - 1, j - 1))
    in_specs = [pl.BlockSpec((tm, D_MODEL), lambda i, j: (i, 0)),
                pl.BlockSpec((1, 3, D_MODEL), lambda i, j: (row(i), 0, 0)),
                pl.BlockSpec((1, D_MODEL), lambda i, j: (0, 0)),
                pl.BlockSpec((D_MODEL, tn), lambda i, j: (0, j)),
                pl.BlockSpec((1, C_HEAD_DIM), lambda i, j: (0, 0)),
                pl.BlockSpec((1, C_HEAD_DIM), lambda i, j: (0, 0))]
    out_shape = [jax.ShapeDtypeStruct((m, L1_IN), BF16)]
    out_specs = [pl.BlockSpec((tm, tn), lambda i, j: (i, out_col(j)))]
    if emit_kv:
        out_shape += [jax.ShapeDtypeStruct((m, C_KV_WIDTH), F32)] * 2
        out_specs += [pl.BlockSpec((tm, C_KV_WIDTH), lambda i, j: (i, 0))] * 2
    return pl.pallas_call(
        functools.partial(_inproj1_kernel, tn=tn, emit_kv=emit_kv),
        out_shape=tuple(out_shape),
        grid=(m // tm, nblk),
        in_specs=in_specs,
        out_specs=tuple(out_specs),
        scratch_shapes=[pltpu.VMEM((tm, D_MODEL), BF16)],
        compiler_params=_params(("parallel", "arbitrary")),
        name="l1_inproj",
    )(x, mod, g, w, q_norm, k_norm)


def _rope(x_bf16, cos, sin_signed):
    x = x_bf16.astype(F32)
    lane = lax.broadcasted_iota(jnp.int32, x.shape, 1)
    swapped = jnp.where((lane // 32) % 2 == 0, pltpu.roll(x, C_HEAD_DIM - 32, axis=1), pltpu.roll(x, 32, axis=1))
    return (x * cos + swapped * sin_signed).astype(BF16)


def _attn_kernel(*refs, has_ctx, use_rope):
    it = iter(refs)
    q_ref, z_ref, kn_ref, vn_ref = next(it), next(it), next(it), next(it)
    kc_ref = next(it) if has_ctx else None
    vc_ref = next(it) if has_ctx else None
    if use_rope:
        cosq_ref, sinq_ref, cosk_ref, sink_ref = next(it), next(it), next(it), next(it)
    o_ref, kk_ref, vv_ref = next(it), next(it), next(it)
    past = kc_ref.shape[1] if has_ctx else 0
    n_new = kn_ref.shape[0]
    rep = C_HEADS // C_KV_HEADS
    nt = (((1,), (1,)), ((), ()))

    @pl.when(pl.program_id(1) == 0)
    def _():
        for kv in range(C_KV_HEADS):
            kcols = slice(kv * C_HEAD_DIM, (kv + 1) * C_HEAD_DIM)
            if has_ctx:
                kk_ref[kv, 0:past, :] = kc_ref[0, :, kcols]
                vv_ref[kv, 0:past, 0:C_HEAD_DIM] = vc_ref[0, :, kcols]
            k_new = kn_ref[:, kcols]
            kk_ref[kv, past:past + n_new, :] = _rope(k_new, cosk_ref[...], sink_ref[...]) if use_rope else k_new
            vv_ref[kv, past:past + n_new, 0:C_HEAD_DIM] = vn_ref[:, kcols]
            vv_ref[kv, :, C_HEAD_DIM:2 * C_HEAD_DIM] = jnp.ones((past + n_new, C_HEAD_DIM), BF16)

    for kv in range(C_KV_HEADS):
        for e in range(rep):
            cols = slice((kv * rep + e) * C_HEAD_DIM, (kv * rep + e + 1) * C_HEAD_DIM)
            q = _rope(q_ref[:, cols], cosq_ref[...], sinq_ref[...]) if use_rope else q_ref[:, cols]
            s = lax.dot_general(q, kk_ref[kv], nt, preferred_element_type=F32)
            mx = jnp.max(s, axis=-1, keepdims=True)
            o = jnp.dot(jnp.exp2(s - mx).astype(BF16), vv_ref[kv], preferred_element_type=F32)
            out = o[:, :C_HEAD_DIM] * (1.0 / o[:, C_HEAD_DIM:]) * z_ref[:, cols].astype(F32)
            o_ref[:, cols] = out.astype(BF16)


def _attention(qkvz, seq_len, n_batch, tq, ctx_k=None, ctx_v=None, rope=None):
    m = qkvz.shape[0]
    qb = seq_len // tq
    kcol = (2 * C_WIDTH) // C_KV_WIDTH
    has_ctx = ctx_k is not None
    use_rope = rope is not None
    past = ctx_k.shape[1] if has_ctx else 0
    in_specs = [pl.BlockSpec((tq, C_WIDTH), lambda b, i: (b * qb + i, 0)),
                pl.BlockSpec((tq, C_WIDTH), lambda b, i: (b * qb + i, 1)),
                pl.BlockSpec((seq_len, C_KV_WIDTH), lambda b, i: (b, kcol)),
                pl.BlockSpec((seq_len, C_KV_WIDTH), lambda b, i: (b, kcol + 1))]
    args = [qkvz, qkvz, qkvz, qkvz]
    if has_ctx:
        in_specs += [pl.BlockSpec((1, past, C_KV_WIDTH), lambda b, i: (b, 0, 0))] * 2
        args += [ctx_k, ctx_v]
    if use_rope:
        in_specs += [pl.BlockSpec((tq, C_HEAD_DIM), lambda b, i: (i, 0))] * 2
        in_specs += [pl.BlockSpec((seq_len, C_HEAD_DIM), lambda b, i: (0, 0))] * 2
        args += [*rope, *rope]
    return pl.pallas_call(
        functools.partial(_attn_kernel, has_ctx=has_ctx, use_rope=use_rope),
        out_shape=jax.ShapeDtypeStruct((m, C_WIDTH), BF16),
        grid=(n_batch, qb),
        in_specs=in_specs,
        out_specs=pl.BlockSpec((tq, C_WIDTH), lambda b, i: (b * qb + i, 0)),
        scratch_shapes=[pltpu.VMEM((C_KV_HEADS, past + seq_len, C_HEAD_DIM), BF16),
                        pltpu.VMEM((C_KV_HEADS, past + seq_len, 2 * C_HEAD_DIM), BF16)],
        compiler_params=_params(("parallel", "arbitrary")),
        name="l1_attention",
    )(*args)


def _expand_matrix(first_row):
    e = np.zeros((2 * CHUNK, B_WIDTH), np.float32)
    ch = np.arange(B_WIDTH)
    e[first_row + ch // B_HEAD_DIM, ch] = 1.0
    e[CHUNK + first_row + ch // B_HEAD_DIM, ch] = 1.0
    return jnp.asarray(e, BF16)


def _rope_tables(n):
    t = np.arange(n)
    pos = np.stack([t // GRID_W, t % GRID_W], axis=1).astype(np.float64)
    half = C_HEAD_DIM // 2
    inv = ROPE_THETA ** (-np.arange(0, half, 2, dtype=np.float64) / half)
    ang = pos[:, :, None] * inv[None, None, :]
    cos = np.concatenate([np.cos(ang), np.cos(ang)], axis=-1).reshape(n, C_HEAD_DIM)
    sin = np.concatenate([-np.sin(ang), np.sin(ang)], axis=-1).reshape(n, C_HEAD_DIM)
    return jnp.asarray(cos, F32), jnp.asarray(sin, F32)


def _pad_lanes(x, width=128):
    return jnp.pad(x, ((0, 0), (0, width - x.shape[1])))


def kernel(x_prompt, x_sample, state_l0_ssm_fwd, state_l0_ssm_bwd, cache_l1_k, cache_l1_v, c, c_ctx, mod_w0, mod_b0, norm_pre0, norm_post0, l0_w_in, l0_v_gain, l0_w_s, l0_b_s, l0_conv_w, l0_conv_b, l0_dt_bias, l0_a_log, l0_d_skip, l0_ssm_norm, l0_w_out, mod_w1, mod_b1, norm_pre1, norm_post1, l1_w_in, l1_q_norm, l1_k_norm, l1_w_out):
    pb, pn, d = x_prompt.shape
    sb, sn, _ = x_sample.shape
    xp = x_prompt.reshape(pb * pn, d)
    xs = x_sample.reshape(sb * sn, d)
    row = lambda v: v.reshape(1, -1)

    cond = jnp.zeros((MOD_ROWS, d), F32).at[:sb].set(c).at[CTX_ROW].set(c_ctx)
    w0_main = l0_w_in.astype(BF16)
    w0_dt = _pad_lanes(l0_w_in[:, L0_MAIN:]).astype(BF16)
    w0_out = l0_w_out.astype(BF16)
    ws = l0_w_s.astype(BF16)
    bs = jnp.broadcast_to(l0_b_s[:, :, None], (A_GROUPS, CHUNK, CHUNK))
    conv_w = jnp.pad(l0_conv_w, ((0, 8 - B_CONV), (0, 0)))
    dt_bias = _pad_lanes(l0_dt_bias.reshape(1, 2 * B_HEADS))
    a_log = _pad_lanes(l0_a_log.reshape(1, 2 * B_HEADS))
    d_skip = jnp.repeat(l0_d_skip, B_HEAD_DIM, axis=1)
    e2f, e2b = _expand_matrix(0), _expand_matrix(B_HEADS)
    w1_in = l1_w_in.astype(BF16)
    w1_out = l1_w_out.astype(BF16)
    ctx_k = cache_l1_k.reshape(sb, -1, C_KV_WIDTH).astype(BF16)
    ctx_v = cache_l1_v.reshape(sb, -1, C_KV_WIDTH).astype(BF16)
    rope = _rope_tables(sn)

    mod0 = _modulation(cond, mod_w0, mod_b0)
    mod1 = _modulation(cond, mod_w1, mod_b1)

    def layer0(x, rows_per_cond, n_batch, seq, h0f, h0b, emit_final):
        n_chunks = seq // CHUNK
        p, dt_raw = _inproj0(x, mod0, row(norm_pre0), w0_main, w0_dt, conv_w, row(l0_conv_b),
                             rows_per_cond, seq)
        bwd = _ssd_bwd(p, dt_raw, dt_bias, a_log, e2b, row(l0_v_gain), ws, bs, h0b,
                       n_batch, n_chunks, emit_final, cps=min(4, n_chunks))
        fwd = _ssd_fwd(p, bwd[1], bwd[2], bwd[0], e2f, e2b, d_skip, row(l0_ssm_norm),
                       h0f, n_batch, n_chunks, emit_final)
        x1 = _outproj([bwd[3], fwd[0]], w0_out, x, mod0, row(norm_post0), rows_per_cond)
        return x1, (fwd[1] if emit_final else None), (bwd[4] if emit_final else None)

    st_shape = (sb, B_WIDTH, B_STATE)
    xp1, new_f, new_b = layer0(xp, None, pb, pn, None, None, True)
    xs1, _, _ = layer0(xs, sn, sb, sn, state_l0_ssm_fwd.reshape(st_shape),
                       state_l0_ssm_bwd.reshape(st_shape), False)

    qp, new_k, new_v = _inproj1(xp1, mod1, row(norm_pre1), w1_in, row(l1_q_norm), row(l1_k_norm),
                                None, True)
    op = _attention(qp, pn, pb, pn)
    yp = _outproj([op], w1_out, xp1, mod1, row(norm_post1), None)

    (qs,) = _inproj1(xs1, mod1, row(norm_pre1), w1_in, row(l1_q_norm), row(l1_k_norm), sn, False)
    os_ = _attention(qs, sn, sb, min(sn, 512), ctx_k, ctx_v, rope)
    ys = _outproj([os_], w1_out, xs1, mod1, row(norm_post1), sn)

    return (yp.reshape(pb, pn, d), ys.reshape(sb, sn, d),
            new_f.reshape(pb, B_HEADS, B_HEAD_DIM, B_STATE),
            new_b.reshape(pb, B_HEADS, B_HEAD_DIM, B_STATE),
            new_k.reshape(pb, pn, C_KV_HEADS, C_HEAD_DIM),
            new_v.reshape(pb, pn, C_KV_HEADS, C_HEAD_DIM))
```

```python
import functools

import numpy as np
import jax
import jax.numpy as jnp
from jax import lax
from jax.experimental import pallas as pl
from jax.experimental.pallas import tpu as pltpu

F32 = jnp.float32
BF16 = jnp.bfloat16

EPS = 1e-6
LOG2E = 1.4426950408889634
GELU_K = 0.7978845608028654

D_MODEL = 2048
CHUNK = 128
GRID_W = 64
ROPE_THETA = 10000.0
A_WIDTH = 2048
A_GROUPS = 16
B_WIDTH = 2048
B_HEAD_DIM = 64
B_HEADS = 32
B_GROUPS = 8
B_STATE = 128
B_CONV = 5
B_GN = B_GROUPS * B_STATE
B_CONV_CH = B_WIDTH + 2 * B_GN
B_GROUP_CH = B_WIDTH // B_GROUPS
L0_MAIN = 2 * A_WIDTH + A_WIDTH + B_WIDTH + B_CONV_CH
C_HEADS = 16
C_KV_HEADS = 4
C_HEAD_DIM = 128
C_WIDTH = 2048
C_KV_WIDTH = 512
L1_IN = C_WIDTH + 2 * C_KV_WIDTH + C_WIDTH

MOD_ROWS = 16
CTX_ROW = 8
V7X_VMEM_LIMIT = 56 * 1024 * 1024
SUB = 256
GAP = 16


def _silu(x):
    return x * (1.0 / (1.0 + jnp.exp(-x)))


def _silu_tanh(x):
    return x * (0.5 + 0.5 * jnp.tanh(0.5 * x))


def _softplus(x):
    return jnp.maximum(x, 0.0) + jnp.log1p(jnp.exp(-jnp.abs(x)))


def _rms_rows(x):
    return x * lax.rsqrt(jnp.mean(x * x, axis=-1, keepdims=True) + EPS)


def _params(sem, vmem=V7X_VMEM_LIMIT):
    return pltpu.CompilerParams(dimension_semantics=sem, vmem_limit_bytes=vmem)


CAST_BLOCK_BYTES = 8 * 1024 * 1024


def _cast_kernel(x_ref, o_ref):
    o_ref[...] = x_ref[...].astype(BF16)


def _to_bf16(w):
    rows, cols = w.shape
    tr = rows
    while tr > 16 and tr * cols * 4 > CAST_BLOCK_BYTES and rows % (tr // 2) == 0:
        tr //= 2
    return pl.pallas_call(
        _cast_kernel,
        out_shape=jax.ShapeDtypeStruct((rows, cols), BF16),
        grid=(rows // tr,),
        in_specs=[pl.BlockSpec((tr, cols), lambda i: (i, 0))],
        out_specs=pl.BlockSpec((tr, cols), lambda i: (i, 0)),
        compiler_params=_params(("parallel",)),
        name="cast_bf16",
    )(w)


def _mod_kernel(c_ref, w_ref, b_ref, o_ref):
    s = _silu(c_ref[...]).astype(BF16)
    o_ref[...] = jnp.dot(s, w_ref[...].astype(BF16), preferred_element_type=F32) + b_ref[...]


def _modulation(cond, w, b, tn=1024):
    n = w.shape[1]
    m = pl.pallas_call(
        _mod_kernel,
        out_shape=jax.ShapeDtypeStruct((MOD_ROWS, n), F32),
        grid=(n // tn,),
        in_specs=[pl.BlockSpec((MOD_ROWS, D_MODEL), lambda j: (0, 0)),
                  pl.BlockSpec((D_MODEL, tn), lambda j: (0, j)),
                  pl.BlockSpec((1, tn), lambda j: (0, j))],
        out_specs=pl.BlockSpec((MOD_ROWS, tn), lambda j: (0, j)),
        compiler_params=_params(("arbitrary",)),
        name="modulation",
    )(cond, w, b.reshape(1, n))
    return m.reshape(MOD_ROWS, 3, D_MODEL)


def _mod_row_map(rows_per_cond, tm):
    if rows_per_cond is None:
        return lambda i: CTX_ROW
    return lambda i: (i * tm) // rows_per_cond


def _prenorm(x, mod_ref, g_ref):
    gain = g_ref[...] * (1.0 + mod_ref[0, 1:2, :])
    r = lax.rsqrt(jnp.mean(x * x, axis=-1, keepdims=True) + EPS)
    return (x * r * gain + mod_ref[0, 0:1, :]).astype(BF16)


def _inproj0_kernel(xp_ref, x_ref, xn_ref, mod_ref, g_ref, w_ref, wdt_ref, cw_ref, cb_ref,
                    p_ref, dt_ref, h_ref, hg_ref, acc_ref, *, tm, tn, seq, n_gelu, n_gated):
    i = pl.program_id(0)
    j = pl.program_id(1)
    nseg = max(1, tm // seq)
    seg = tm // nseg
    rows_g = hg_ref.shape[0]

    @pl.when(j == 0)
    def _():
        h = _prenorm(x_ref[...], mod_ref, g_ref)
        h_ref[...] = h
        dt_ref[...] = jnp.dot(h, wdt_ref[...], preferred_element_type=F32)
        zero = jnp.zeros((GAP, D_MODEL), BF16)
        has_prev = (i * tm) % seq != 0
        has_next = ((i + 1) * tm) % seq != 0
        hg_ref[0:GAP, :] = jnp.where(has_prev, _prenorm(xp_ref[...], mod_ref, g_ref), zero)
        for q in range(nseg):
            base = GAP + q * (seg + GAP)
            hg_ref[base:base + seg, :] = h[q * seg:(q + 1) * seg]
            if q < nseg - 1:
                hg_ref[base + seg:base + seg + GAP, :] = zero
        hg_ref[rows_g - GAP:rows_g, :] = jnp.where(has_next, _prenorm(xn_ref[...], mod_ref, g_ref), zero)

    @pl.when(j < n_gelu)
    def _():
        for s in range(tn // SUB):
            sub = slice(s * SUB, (s + 1) * SUB)
            acc = jnp.dot(h_ref[...], w_ref[:, sub], preferred_element_type=F32)
            t = jnp.tanh(acc * (GELU_K + (GELU_K * 0.044715) * (acc * acc)))
            p_ref[:, sub] = (acc * (0.5 + 0.5 * t)).astype(BF16)

    @pl.when((j >= n_gelu) & (j < n_gated))
    def _():
        for s in range(tn // SUB):
            sub = slice(s * SUB, (s + 1) * SUB)
            acc = jnp.dot(h_ref[...], w_ref[:, sub], preferred_element_type=F32)
            p_ref[:, sub] = _silu_tanh(acc).astype(BF16)

    @pl.when(j >= n_gated)
    def _():
        for s in range(tn // SUB):
            sub = slice(s * SUB, (s + 1) * SUB)
            buf = s % 2
            acc_ref[buf] = jnp.dot(hg_ref[...], w_ref[:, sub], preferred_element_type=F32)
            for q in range(nseg):
                base = GAP + q * (seg + GAP)
                out = None
                for k in range(B_CONV):
                    tap = acc_ref[buf, pl.ds(base - B_CONV // 2 + k, seg), :]
                    term = tap * cw_ref[k:k + 1, sub]
                    out = term if out is None else out + term
                p_ref[q * seg:(q + 1) * seg, sub] = _silu_tanh(out + cb_ref[:, sub]).astype(BF16)


def _inproj0(x, mod, g, w_main, w_dt, conv_w, conv_b, rows_per_cond, seq, tm=1024, tn=1024):
    m = x.shape[0]
    row = _mod_row_map(rows_per_cond, tm)
    nseg = max(1, tm // seq)
    rows_g = tm + (nseg + 1) * GAP
    n_gated = (L0_MAIN - B_CONV_CH) // tn
    gb = tm // GAP
    last_gap = m // GAP - 1
    conv_col = lambda j: jnp.maximum(j - n_gated, 0)
    kern = functools.partial(_inproj0_kernel, tm=tm, tn=tn, seq=seq, n_gelu=2 * A_WIDTH // tn, n_gated=n_gated)
    return pl.pallas_call(
        kern,
        out_shape=(jax.ShapeDtypeStruct((m, L0_MAIN), BF16),
                   jax.ShapeDtypeStruct((m, 128), F32)),
        grid=(m // tm, L0_MAIN // tn),
        in_specs=[pl.BlockSpec((GAP, D_MODEL), lambda i, j: (jnp.maximum(i * gb - 1, 0), 0)),
                  pl.BlockSpec((tm, D_MODEL), lambda i, j: (i, 0)),
                  pl.BlockSpec((GAP, D_MODEL), lambda i, j: (jnp.minimum((i + 1) * gb, last_gap), 0)),
                  pl.BlockSpec((1, 3, D_MODEL), lambda i, j: (row(i), 0, 0)),
                  pl.BlockSpec((1, D_MODEL), lambda i, j: (0, 0)),
                  pl.BlockSpec((D_MODEL, tn), lambda i, j: (0, j)),
                  pl.BlockSpec((D_MODEL, 128), lambda i, j: (0, 0)),
                  pl.BlockSpec((8, tn), lambda i, j: (0, conv_col(j))),
                  pl.BlockSpec((1, tn), lambda i, j: (0, conv_col(j)))],
        out_specs=(pl.BlockSpec((tm, tn), lambda i, j: (i, j)),
                   pl.BlockSpec((tm, 128), lambda i, j: (i, 0))),
        scratch_shapes=[pltpu.VMEM((tm, D_MODEL), BF16),
                        pltpu.VMEM((rows_g, D_MODEL), BF16),
                        pltpu.VMEM((2, rows_g, SUB), F32)],
        compiler_params=_params(("parallel", "arbitrary")),
        name="l0_inproj",
    )(x, x, x, mod, g, w_main, w_dt, conv_w, conv_b)


P_U, P_V, P_ZA, P_ZB, P_X = 0, 1, 2, 3, 4
P_BC = 5
P_B = (L0_MAIN - 2 * B_GN) // B_GN


def _mixa_kernel(u_ref, v_ref, z_ref, vg_ref, ws_ref, bs_ref, o_ref, *, n_chunks):
    v = v_ref[...].astype(F32)
    vc = v - jnp.mean(v, axis=-1, keepdims=True)
    vn = vc * lax.rsqrt(jnp.mean(vc * vc, axis=-1, keepdims=True) + EPS) * vg_ref[...]
    vn = vn.astype(BF16)
    gd = A_WIDTH // A_GROUPS
    for c in range(n_chunks):
        rows = slice(c * CHUNK, (c + 1) * CHUNK)
        for g in range(A_GROUPS):
            cols = slice(g * gd, (g + 1) * gd)
            s = jnp.dot(ws_ref[g], vn[rows, cols], preferred_element_type=F32) + bs_ref[g]
            o = u_ref[rows, cols].astype(F32) * s * z_ref[rows, cols].astype(F32)
            o_ref[rows, cols] = o.astype(BF16)


def _split2(x):
    hi = x.astype(BF16)
    lo = (x - hi.astype(F32)).astype(BF16)
    return jnp.concatenate([hi, lo], axis=1)


def _split3(x):
    hi = x.astype(BF16)
    r = x - hi.astype(F32)
    mid = r.astype(BF16)
    lo = (r - mid.astype(F32)).astype(BF16)
    return hi, mid, lo


def _tri_matmul(tri, x):
    hi, mid, lo = _split3(x)
    out = jnp.dot(tri, lo, preferred_element_type=F32)
    out = out + jnp.dot(tri, mid, preferred_element_type=F32)
    return out + jnp.dot(tri, hi, preferred_element_type=F32)


def _expand(x_split, e2):
    return jnp.dot(x_split, e2, preferred_element_type=F32)


def _ssd_small(dt_raw, dtb_ref, alog_ref):
    dt = _softplus(dt_raw + dtb_ref[...])
    da = dt * (-jnp.exp(alog_ref[...]))
    r = lax.broadcasted_iota(jnp.int32, (CHUNK, CHUNK), 0)
    c = lax.broadcasted_iota(jnp.int32, (CHUNK, CHUNK), 1)
    lower = jnp.where(c <= r, 1.0, 0.0).astype(BF16)
    upper = jnp.where(c >= r, 1.0, 0.0).astype(BF16)
    cum = jnp.where(c < B_HEADS, _tri_matmul(lower, da), _tri_matmul(upper, da))
    return dt, cum, r, c


def _init_state(state_ref, h0_ref):
    for g in range(B_GROUPS):
        if h0_ref is not None:
            state_ref[g] = h0_ref[0, g * B_GROUP_CH:(g + 1) * B_GROUP_CH, :].T
        else:
            state_ref[g] = jnp.zeros((B_STATE, B_GROUP_CH), F32)


def _emit_state(hfin_ref, state_ref):
    for g in range(B_GROUPS):
        hfin_ref[0, g * B_GROUP_CH:(g + 1) * B_GROUP_CH, :] = state_ref[g].T


def _ssd_bwd_kernel(*refs, n_chunks, has_init, emit_final):
    it = iter(refs)
    x_ref, b_ref, dtraw_ref, dtb_ref, alog_ref, e2b_ref = (next(it), next(it), next(it), next(it),
                                                          next(it), next(it))
    mixa_in = [next(it) for _ in range(6)]
    h0_ref = next(it) if has_init else None
    hin_ref, sm_ref, smb_ref, a_ref = next(it), next(it), next(it), next(it)
    hfin_ref = next(it) if emit_final else None
    hb_ref = next(it)

    j = pl.program_id(1)
    cps = hin_ref.shape[0]

    @pl.when(j == 0)
    def _():
        _init_state(hb_ref, h0_ref)

    _mixa_kernel(*mixa_in, a_ref, n_chunks=cps)

    xws, tots = [], []
    for lc in range(cps):
        rows = slice(lc * CHUNK, (lc + 1) * CHUNK)
        dt, cum, _, lane = _ssd_small(dtraw_ref[rows, :], dtb_ref, alog_ref)
        is_bwd = (lane >= B_HEADS) & (lane < 2 * B_HEADS)
        w_b = jnp.where(is_bwd, dt * jnp.exp(jnp.where(is_bwd, cum[0:1, :] - cum, 0.0)), 0.0)
        xws.append((x_ref[rows, :].astype(F32) * _expand(_split2(w_b), e2b_ref[...])).astype(BF16))
        tot_split = _split2(jnp.broadcast_to(jnp.exp(cum[0:1, :]), (GAP, CHUNK)))
        tots.append(_expand(tot_split, e2b_ref[...])[0:1, :])
        cum2 = cum * LOG2E
        ldt2 = jnp.log(dt) * LOG2E
        sm_ref[rows, 0:CHUNK] = cum2
        sm_ref[rows, CHUNK:2 * CHUNK] = (ldt2 - cum2).T
        sm_ref[rows, 2 * CHUNK:3 * CHUNK] = ldt2.T
        is_fwd = lane < B_HEADS
        to_end = jnp.exp(jnp.where(is_fwd, cum[CHUNK - 1:CHUNK, :] - cum, 0.0))
        smb_ref[rows, 0:2 * CHUNK] = _split2(jnp.exp(cum))
        smb_ref[rows, 2 * CHUNK:4 * CHUNK] = _split2(jnp.where(is_fwd, dt * to_end, 0.0))

    for g in range(B_GROUPS):
        cols = slice(g * B_GROUP_CH, (g + 1) * B_GROUP_CH)
        hb_g = hb_ref[g]
        for lc in reversed(range(cps)):
            rows = slice(lc * CHUNK, (lc + 1) * CHUNK)
            hin_ref[lc, g] = hb_g.astype(BF16)
            upd = lax.dot_general(b_ref[rows, g * B_STATE:(g + 1) * B_STATE], xws[lc][:, cols],
                                  (((0,), (0,)), ((), ())), preferred_element_type=F32)
            hb_g = hb_g * tots[lc][:, cols] + upd
        hb_ref[g] = hb_g

    if emit_final:
        @pl.when(j == pl.num_programs(1) - 1)
        def _():
            _emit_state(hfin_ref, hb_ref)


def _ssd_bwd(p, dt_raw, dt_bias, a_log, e2b, v_gain, w_s, b_s, h0, n_batch, n_chunks, emit_final, cps=2):
    has_init = h0 is not None
    steps = n_chunks // cps
    rows = cps * CHUNK
    blk = lambda b, j: b * steps + (steps - 1 - j)
    const = lambda b, j: (0, 0)
    in_specs = [
        pl.BlockSpec((rows, B_WIDTH), lambda b, j: (blk(b, j), P_X)),
        pl.BlockSpec((rows, B_GN), lambda b, j: (blk(b, j), P_B)),
        pl.BlockSpec((rows, 128), lambda b, j: (blk(b, j), 0)),
        pl.BlockSpec((1, 128), const),
        pl.BlockSpec((1, 128), const),
        pl.BlockSpec((2 * CHUNK, B_WIDTH), const),
        pl.BlockSpec((rows, A_WIDTH), lambda b, j: (blk(b, j), P_U)),
        pl.BlockSpec((rows, A_WIDTH), lambda b, j: (blk(b, j), P_V)),
        pl.BlockSpec((rows, A_WIDTH), lambda b, j: (blk(b, j), P_ZA)),
        pl.BlockSpec((1, A_WIDTH), const),
        pl.BlockSpec((A_GROUPS, CHUNK, CHUNK), lambda b, j: (0, 0, 0)),
        pl.BlockSpec((A_GROUPS, CHUNK, CHUNK), lambda b, j: (0, 0, 0)),
    ]
    args = [p, p, dt_raw, dt_bias, a_log, e2b, p, p, p, v_gain, w_s, b_s]
    if has_init:
        in_specs.append(pl.BlockSpec((1, B_WIDTH, B_STATE), lambda b, j: (b, 0, 0)))
        args.append(h0)
    m = n_batch * n_chunks * CHUNK
    out_shape = [jax.ShapeDtypeStruct((n_batch * n_chunks, B_GROUPS, B_STATE, B_GROUP_CH), BF16),
                 jax.ShapeDtypeStruct((m, 3 * CHUNK), F32),
                 jax.ShapeDtypeStruct((m, 4 * CHUNK), BF16),
                 jax.ShapeDtypeStruct((m, A_WIDTH), BF16)]
    out_specs = [pl.BlockSpec((cps, B_GROUPS, B_STATE, B_GROUP_CH), lambda b, j: (blk(b, j), 0, 0, 0)),
                 pl.BlockSpec((rows, 3 * CHUNK), lambda b, j: (blk(b, j), 0)),
                 pl.BlockSpec((rows, 4 * CHUNK), lambda b, j: (blk(b, j), 0)),
                 pl.BlockSpec((rows, A_WIDTH), lambda b, j: (blk(b, j), 0))]
    if emit_final:
        out_shape.append(jax.ShapeDtypeStruct((n_batch, B_WIDTH, B_STATE), F32))
        out_specs.append(pl.BlockSpec((1, B_WIDTH, B_STATE), lambda b, j: (b, 0, 0)))
    return pl.pallas_call(
        functools.partial(_ssd_bwd_kernel, n_chunks=n_chunks, has_init=has_init, emit_final=emit_final),
        out_shape=tuple(out_shape),
        grid=(n_batch, steps),
        in_specs=in_specs,
        out_specs=tuple(out_specs),
        scratch_shapes=[pltpu.VMEM((B_GROUPS, B_STATE, B_GROUP_CH), F32)],
        compiler_params=_params(("parallel", "arbitrary")),
        name="l0_ssd_bwd",
    )(*args)


def _ssd_fwd_kernel(*refs, n_chunks, has_init, emit_final):
    it = iter(refs)
    x_ref, bc_ref, sm_ref, smb_ref, z_ref, hin_ref = (next(it), next(it), next(it), next(it),
                                                      next(it), next(it))
    e2f_ref, e2b_ref, dsk_ref, ng_ref = next(it), next(it), next(it), next(it)
    h0_ref = next(it) if has_init else None
    y_ref = next(it)
    hfin_ref = next(it) if emit_final else None
    hf_ref = next(it)

    j = pl.program_id(1)

    @pl.when(j == 0)
    def _():
        _init_state(hf_ref, h0_ref)

    cps = hin_ref.shape[0]
    r = lax.broadcasted_iota(jnp.int32, (CHUNK, CHUNK), 0)
    c = lax.broadcasted_iota(jnp.int32, (CHUNK, CHUNK), 1)
    dsk = dsk_ref[0:1, :] + dsk_ref[1:2, :]
    heads_per_group = B_HEADS // B_GROUPS
    col_head = lax.broadcasted_iota(jnp.int32, (CHUNK, B_GROUP_CH), 1) // B_HEAD_DIM

    for g in range(B_GROUPS):
        cols = slice(g * B_GROUP_CH, (g + 1) * B_GROUP_CH)
        e2f = e2f_ref[:, cols]
        e2b = e2b_ref[:, cols]
        hf_g = hf_ref[g]
        for lc in range(cps):
            rows = slice(lc * CHUNK, (lc + 1) * CHUNK)
            cum2 = sm_ref[rows, 0:CHUNK]
            r2t = sm_ref[rows, CHUNK:2 * CHUNK]
            ldt2t = sm_ref[rows, 2 * CHUNK:3 * CHUNK]
            dec_split = smb_ref[rows, 0:2 * CHUNK]
            dec_f = _expand(dec_split, e2f)
            dec_b = _expand(dec_split, e2b)
            b_g = bc_ref[rows, g * B_STATE:(g + 1) * B_STATE]
            c_g = bc_ref[rows, B_GN + g * B_STATE:B_GN + (g + 1) * B_STATE]
            cb = lax.dot_general(c_g, b_g, (((1,), (1,)), ((), ())), preferred_element_type=F32)
            cb_f = jnp.where(c <= r, cb, 0.0)
            cb_b = jnp.where(c >= r, cb, 0.0)
            y_g = jnp.dot(c_g, hf_g.astype(BF16), preferred_element_type=F32) * dec_f
            y_g = y_g + jnp.dot(c_g, hin_ref[lc, g], preferred_element_type=F32) * dec_b
            x_g = x_ref[rows, cols]
            y_g = y_g + dsk[:, cols] * x_g.astype(F32)
            mats, x_blocks = [], []
            for e in range(heads_per_group):
                h = g * heads_per_group + e
                hb = B_HEADS + h
                arg_f = jnp.minimum(cum2[:, h:h + 1] + r2t[h:h + 1, :], ldt2t[h:h + 1, :])
                arg_b = jnp.minimum(cum2[:, hb:hb + 1] + r2t[hb:hb + 1, :], ldt2t[hb:hb + 1, :])
                mats.append((cb_f * jnp.exp2(arg_f) + cb_b * jnp.exp2(arg_b)).astype(BF16))
                x_blocks.append(jnp.where(col_head == e, x_g, jnp.zeros_like(x_g)))
            y_g = y_g + jnp.dot(jnp.concatenate(mats, axis=1), jnp.concatenate(x_blocks, axis=0),
                                preferred_element_type=F32)
            xw = (x_g.astype(F32) * _expand(smb_ref[rows, 2 * CHUNK:4 * CHUNK], e2f)).astype(BF16)
            upd = lax.dot_general(b_g, xw, (((0,), (0,)), ((), ())), preferred_element_type=F32)
            hf_g = hf_g * dec_f[CHUNK - 1:CHUNK, :] + upd
            yz = y_g * z_ref[rows, cols].astype(F32)
            y_ref[rows, cols] = (_rms_rows(yz) * ng_ref[:, cols]).astype(BF16)
        hf_ref[g] = hf_g

    if emit_final:
        @pl.when(j == pl.num_programs(1) - 1)
        def _():
            _emit_state(hfin_ref, hf_ref)


def _ssd_fwd(p, small, small_bf, hin, e2f, e2b, d_skip, norm_g, h0, n_batch, n_chunks, emit_final, cps=2):
    m = p.shape[0]
    has_init = h0 is not None
    steps = n_chunks // cps
    rows = cps * CHUNK
    blk = lambda b, j: b * steps + j
    const = lambda b, j: (0, 0)
    in_specs = [
        pl.BlockSpec((rows, B_WIDTH), lambda b, j: (blk(b, j), P_X)),
        pl.BlockSpec((rows, 2 * B_GN), lambda b, j: (blk(b, j), P_BC)),
        pl.BlockSpec((rows, 3 * CHUNK), lambda b, j: (blk(b, j), 0)),
        pl.BlockSpec((rows, 4 * CHUNK), lambda b, j: (blk(b, j), 0)),
        pl.BlockSpec((rows, B_WIDTH), lambda b, j: (blk(b, j), P_ZB)),
        pl.BlockSpec((cps, B_GROUPS, B_STATE, B_GROUP_CH), lambda b, j: (blk(b, j), 0, 0, 0)),
        pl.BlockSpec((2 * CHUNK, B_WIDTH), const),
        pl.BlockSpec((2 * CHUNK, B_WIDTH), const),
        pl.BlockSpec((2, B_WIDTH), const),
        pl.BlockSpec((1, B_WIDTH), const),
    ]
    args = [p, p, small, small_bf, p, hin, e2f, e2b, d_skip, norm_g]
    if has_init:
        in_specs.append(pl.BlockSpec((1, B_WIDTH, B_STATE), lambda b, j: (b, 0, 0)))
        args.append(h0)
    out_shape = [jax.ShapeDtypeStruct((m, B_WIDTH), BF16)]
    out_specs = [pl.BlockSpec((rows, B_WIDTH), lambda b, j: (blk(b, j), 0))]
    if emit_final:
        out_shape.append(jax.ShapeDtypeStruct((n_batch, B_WIDTH, B_STATE), F32))
        out_specs.append(pl.BlockSpec((1, B_WIDTH, B_STATE), lambda b, j: (b, 0, 0)))
    return pl.pallas_call(
        functools.partial(_ssd_fwd_kernel, n_chunks=n_chunks, has_init=has_init, emit_final=emit_final),
        out_shape=tuple(out_shape),
        grid=(n_batch, steps),
        in_specs=in_specs,
        out_specs=tuple(out_specs),
        scratch_shapes=[pltpu.VMEM((B_GROUPS, B_STATE, B_GROUP_CH), F32)],
        compiler_params=_params(("parallel", "arbitrary")),
        name="l0_ssd_fwd",
    )(*args)


def _outproj_kernel(*refs, n_in):
    acts, ws = refs[:n_in], refs[n_in:2 * n_in]
    x_ref, mod_ref, g_ref, o_ref = refs[2 * n_in:]
    acc = jnp.dot(acts[0][...], ws[0][...], preferred_element_type=F32)
    for a_ref, w_ref in zip(acts[1:], ws[1:]):
        acc = acc + jnp.dot(a_ref[...], w_ref[...], preferred_element_type=F32)
    gain = mod_ref[0, 2:3, :] * g_ref[...]
    r = lax.rsqrt(jnp.mean(acc * acc, axis=-1, keepdims=True) + EPS)
    o_ref[...] = x_ref[...] + acc * r * gain


def _outproj(acts, w, x, mod, g, rows_per_cond, tm=512):
    m = x.shape[0]
    n_in = len(acts)
    kdim = acts[0].shape[1]
    row = _mod_row_map(rows_per_cond, tm)
    in_specs = [pl.BlockSpec((tm, kdim), lambda i: (i, 0)) for _ in acts]
    in_specs += [pl.BlockSpec((kdim, D_MODEL), functools.partial(lambda i, k: (k, 0), k=k),
                              pipeline_mode=pl.Buffered(1)) for k in range(n_in)]
    in_specs += [pl.BlockSpec((tm, D_MODEL), lambda i: (i, 0)),
                 pl.BlockSpec((1, 3, D_MODEL), lambda i: (row(i), 0, 0)),
                 pl.BlockSpec((1, D_MODEL), lambda i: (0, 0))]
    return pl.pallas_call(
        functools.partial(_outproj_kernel, n_in=n_in),
        out_shape=jax.ShapeDtypeStruct((m, D_MODEL), F32),
        grid=(m // tm,),
        in_specs=in_specs,
        out_specs=pl.BlockSpec((tm, D_MODEL), lambda i: (i, 0)),
        compiler_params=_params(("parallel",)),
        name="outproj",
    )(*acts, *([w] * n_in), x, mod, g)


Q_SCALE = C_HEAD_DIM ** -0.5 * LOG2E


def _inproj1_kernel(*refs, tn, emit_kv):
    it = iter(refs)
    x_ref, mod_ref, g_ref, w_ref, qn_ref, kn_ref = next(it), next(it), next(it), next(it), next(it), next(it)
    p_ref = next(it)
    k_ref = next(it) if emit_kv else None
    v_ref = next(it) if emit_kv else None
    h_ref = next(it)

    j = pl.program_id(1)
    nq = C_WIDTH // tn

    @pl.when(j == 0)
    def _():
        h_ref[...] = _prenorm(x_ref[...], mod_ref, g_ref)

    def sub_dot(s):
        return jnp.dot(h_ref[...], w_ref[:, s * SUB:(s + 1) * SUB], preferred_element_type=F32)

    @pl.when(j < nq)
    def _():
        for s in range(tn // SUB):
            acc = sub_dot(s)
            for hh in range(SUB // C_HEAD_DIM):
                q = _rms_rows(acc[:, hh * C_HEAD_DIM:(hh + 1) * C_HEAD_DIM]) * (qn_ref[...] * Q_SCALE)
                col = s * SUB + hh * C_HEAD_DIM
                p_ref[:, col:col + C_HEAD_DIM] = q.astype(BF16)

    @pl.when(j == nq)
    def _():
        for s in range(tn // SUB):
            acc = sub_dot(s)
            if s * SUB < C_KV_WIDTH:
                for hh in range(SUB // C_HEAD_DIM):
                    k = _rms_rows(acc[:, hh * C_HEAD_DIM:(hh + 1) * C_HEAD_DIM]) * kn_ref[...]
                    col = s * SUB + hh * C_HEAD_DIM
                    p_ref[:, col:col + C_HEAD_DIM] = k.astype(BF16)
                    if emit_kv:
                        k_ref[:, col:col + C_HEAD_DIM] = k
            else:
                p_ref[:, s * SUB:(s + 1) * SUB] = acc.astype(BF16)
                if emit_kv:
                    v_ref[:, s * SUB - C_KV_WIDTH:(s + 1) * SUB - C_KV_WIDTH] = acc

    @pl.when(j > nq)
    def _():
        for s in range(tn // SUB):
            p_ref[:, s * SUB:(s + 1) * SUB] = _silu_tanh(sub_dot(s)).astype(BF16)


def _inproj1(x, mod, g, w, q_norm, k_norm, rows_per_cond, emit_kv, tm=1024, tn=1024):
    m = x.shape[0]
    assert tn == 2 * C_KV_WIDTH
    row = _mod_row_map(rows_per_cond, tm)
    nq = C_WIDTH // tn
    nblk = L1_IN // tn
    out_col = lambda j: jnp.where(j < nq, j, jnp.where(j == nq, nblk - 1, j - 1))
    in_specs = [pl.BlockSpec((tm, D_MODEL), lambda i, j: (i, 0)),
                pl.BlockSpec((1, 3, D_MODEL), lambda i, j: (row(i), 0, 0)),
                pl.BlockSpec((1, D_MODEL), lambda i, j: (0, 0)),
                pl.BlockSpec((D_MODEL, tn), lambda i, j: (0, j)),
                pl.BlockSpec((1, C_HEAD_DIM), lambda i, j: (0, 0)),
                pl.BlockSpec((1, C_HEAD_DIM), lambda i, j: (0, 0))]
    out_shape = [jax.ShapeDtypeStruct((m, L1_IN), BF16)]
    out_specs = [pl.BlockSpec((tm, tn), lambda i, j: (i, out_col(j)))]
    if emit_kv:
        out_shape += [jax.ShapeDtypeStruct((m, C_KV_WIDTH), F32)] * 2
        out_specs += [pl.BlockSpec((tm, C_KV_WIDTH), lambda i, j: (i, 0))] * 2
    return pl.pallas_call(
        functools.partial(_inproj1_kernel, tn=tn, emit_kv=emit_kv),
        out_shape=tuple(out_shape),
        grid=(m // tm, nblk),
        in_specs=in_specs,
        out_specs=tuple(out_specs),
        scratch_shapes=[pltpu.VMEM((tm, D_MODEL), BF16)],
        compiler_params=_params(("parallel", "arbitrary")),
        name="l1_inproj",
    )(x, mod, g, w, q_norm, k_norm)


def _rope(x_bf16, cos, sin_signed):
    x = x_bf16.astype(F32)
    lane = lax.broadcasted_iota(jnp.int32, x.shape, 1)
    swapped = jnp.where((lane // 32) % 2 == 0, pltpu.roll(x, C_HEAD_DIM - 32, axis=1), pltpu.roll(x, 32, axis=1))
    return (x * cos + swapped * sin_signed).astype(BF16)


def _attn_kernel(*refs, has_ctx, use_rope):
    it = iter(refs)
    q_ref, z_ref, kn_ref, vn_ref = next(it), next(it), next(it), next(it)
    kc_ref = next(it) if has_ctx else None
    vc_ref = next(it) if has_ctx else None
    if use_rope:
        cosq_ref, sinq_ref, cosk_ref, sink_ref = next(it), next(it), next(it), next(it)
    o_ref, kk_ref, vv_ref = next(it), next(it), next(it)
    past = kc_ref.shape[1] if has_ctx else 0
    n_new = kn_ref.shape[0]
    rep = C_HEADS // C_KV_HEADS
    nt = (((1,), (1,)), ((), ()))

    @pl.when(pl.program_id(1) == 0)
    def _():
        for kv in range(C_KV_HEADS):
            kcols = slice(kv * C_HEAD_DIM, (kv + 1) * C_HEAD_DIM)
            if has_ctx:
                kk_ref[kv, 0:past, :] = kc_ref[0, :, kcols]
                vv_ref[kv, 0:past, 0:C_HEAD_DIM] = vc_ref[0, :, kcols]
            k_new = kn_ref[:, kcols]
            kk_ref[kv, past:past + n_new, :] = _rope(k_new, cosk_ref[...], sink_ref[...]) if use_rope else k_new
            vv_ref[kv, past:past + n_new, 0:C_HEAD_DIM] = vn_ref[:, kcols]
            vv_ref[kv, :, C_HEAD_DIM:2 * C_HEAD_DIM] = jnp.ones((past + n_new, C_HEAD_DIM), BF16)

    for kv in range(C_KV_HEADS):
        for e in range(rep):
            cols = slice((kv * rep + e) * C_HEAD_DIM, (kv * rep + e + 1) * C_HEAD_DIM)
            q = _rope(q_ref[:, cols], cosq_ref[...], sinq_ref[...]) if use_rope else q_ref[:, cols]
            s = lax.dot_general(q, kk_ref[kv], nt, preferred_element_type=F32)
            mx = jnp.max(s, axis=-1, keepdims=True)
            o = jnp.dot(jnp.exp2(s - mx).astype(BF16), vv_ref[kv], preferred_element_type=F32)
            out = o[:, :C_HEAD_DIM] * (1.0 / o[:, C_HEAD_DIM:]) * z_ref[:, cols].astype(F32)
            o_ref[:, cols] = out.astype(BF16)


def _attention(qkvz, seq_len, n_batch, tq, ctx_k=None, ctx_v=None, rope=None):
    m = qkvz.shape[0]
    qb = seq_len // tq
    kcol = (2 * C_WIDTH) // C_KV_WIDTH
    has_ctx = ctx_k is not None
    use_rope = rope is not None
    past = ctx_k.shape[1] if has_ctx else 0
    in_specs = [pl.BlockSpec((tq, C_WIDTH), lambda b, i: (b * qb + i, 0)),
                pl.BlockSpec((tq, C_WIDTH), lambda b, i: (b * qb + i, 1)),
                pl.BlockSpec((seq_len, C_KV_WIDTH), lambda b, i: (b, kcol)),
                pl.BlockSpec((seq_len, C_KV_WIDTH), lambda b, i: (b, kcol + 1))]
    args = [qkvz, qkvz, qkvz, qkvz]
    if has_ctx:
        in_specs += [pl.BlockSpec((1, past, C_KV_WIDTH), lambda b, i: (b, 0, 0))] * 2
        args += [ctx_k, ctx_v]
    if use_rope:
        in_specs += [pl.BlockSpec((tq, C_HEAD_DIM), lambda b, i: (i, 0))] * 2
        in_specs += [pl.BlockSpec((seq_len, C_HEAD_DIM), lambda b, i: (0, 0))] * 2
        args += [*rope, *rope]
    return pl.pallas_call(
        functools.partial(_attn_kernel, has_ctx=has_ctx, use_rope=use_rope),
        out_shape=jax.ShapeDtypeStruct((m, C_WIDTH), BF16),
        grid=(n_batch, qb),
        in_specs=in_specs,
        out_specs=pl.BlockSpec((tq, C_WIDTH), lambda b, i: (b * qb + i, 0)),
        scratch_shapes=[pltpu.VMEM((C_KV_HEADS, past + seq_len, C_HEAD_DIM), BF16),
                        pltpu.VMEM((C_KV_HEADS, past + seq_len, 2 * C_HEAD_DIM), BF16)],
        compiler_params=_params(("parallel", "arbitrary")),
        name="l1_attention",
    )(*args)


def _expand_matrix(first_row):
    e = np.zeros((2 * CHUNK, B_WIDTH), np.float32)
    ch = np.arange(B_WIDTH)
    e[first_row + ch // B_HEAD_DIM, ch] = 1.0
    e[CHUNK + first_row + ch // B_HEAD_DIM, ch] = 1.0
    return jnp.asarray(e, BF16)


def _rope_tables(n):
    t = np.arange(n)
    pos = np.stack([t // GRID_W, t % GRID_W], axis=1).astype(np.float64)
    half = C_HEAD_DIM // 2
    inv = ROPE_THETA ** (-np.arange(0, half, 2, dtype=np.float64) / half)
    ang = pos[:, :, None] * inv[None, None, :]
    cos = np.concatenate([np.cos(ang), np.cos(ang)], axis=-1).reshape(n, C_HEAD_DIM)
    sin = np.concatenate([-np.sin(ang), np.sin(ang)], axis=-1).reshape(n, C_HEAD_DIM)
    return jnp.asarray(cos, F32), jnp.asarray(sin, F32)


def _pad_lanes(x, width=128):
    return jnp.pad(x, ((0, 0), (0, width - x.shape[1])))


def kernel(x_prompt, x_sample, state_l0_ssm_fwd, state_l0_ssm_bwd, cache_l1_k, cache_l1_v, c, c_ctx, mod_w0, mod_b0, norm_pre0, norm_post0, l0_w_in, l0_v_gain, l0_w_s, l0_b_s, l0_conv_w, l0_conv_b, l0_dt_bias, l0_a_log, l0_d_skip, l0_ssm_norm, l0_w_out, mod_w1, mod_b1, norm_pre1, norm_post1, l1_w_in, l1_q_norm, l1_k_norm, l1_w_out):
    pb, pn, d = x_prompt.shape
    sb, sn, _ = x_sample.shape
    xp = x_prompt.reshape(pb * pn, d)
    xs = x_sample.reshape(sb * sn, d)
    row = lambda v: v.reshape(1, -1)

    cond = jnp.zeros((MOD_ROWS, d), F32).at[:sb].set(c).at[CTX_ROW].set(c_ctx)
    w0_main = _to_bf16(l0_w_in)
    w0_dt = _pad_lanes(l0_w_in[:, L0_MAIN:]).astype(BF16)
    w0_out = _to_bf16(l0_w_out)
    ws = l0_w_s.astype(BF16)
    bs = jnp.broadcast_to(l0_b_s[:, :, None], (A_GROUPS, CHUNK, CHUNK))
    conv_w = jnp.pad(l0_conv_w, ((0, 8 - B_CONV), (0, 0)))
    dt_bias = _pad_lanes(l0_dt_bias.reshape(1, 2 * B_HEADS))
    a_log = _pad_lanes(l0_a_log.reshape(1, 2 * B_HEADS))
    d_skip = jnp.repeat(l0_d_skip, B_HEAD_DIM, axis=1)
    e2f, e2b = _expand_matrix(0), _expand_matrix(B_HEADS)
    w1_in = _to_bf16(l1_w_in)
    w1_out = _to_bf16(l1_w_out)
    ctx_k = cache_l1_k.reshape(sb, -1, C_KV_WIDTH).astype(BF16)
    ctx_v = cache_l1_v.reshape(sb, -1, C_KV_WIDTH).astype(BF16)
    rope = _rope_tables(sn)

    mod0 = _modulation(cond, mod_w0, mod_b0)
    mod1 = _modulation(cond, mod_w1, mod_b1)

    def layer0(x, rows_per_cond, n_batch, seq, h0f, h0b, emit_final):
        n_chunks = seq // CHUNK
        p, dt_raw = _inproj0(x, mod0, row(norm_pre0), w0_main, w0_dt, conv_w, row(l0_conv_b),
                             rows_per_cond, seq)
        bwd = _ssd_bwd(p, dt_raw, dt_bias, a_log, e2b, row(l0_v_gain), ws, bs, h0b,
                       n_batch, n_chunks, emit_final, cps=min(4, n_chunks))
        fwd = _ssd_fwd(p, bwd[1], bwd[2], bwd[0], e2f, e2b, d_skip, row(l0_ssm_norm),
                       h0f, n_batch, n_chunks, emit_final, cps=min(4, n_chunks))
        x1 = _outproj([bwd[3], fwd[0]], w0_out, x, mod0, row(norm_post0), rows_per_cond)
        return x1, (fwd[1] if emit_final else None), (bwd[4] if emit_final else None)

    st_shape = (sb, B_WIDTH, B_STATE)
    xp1, new_f, new_b = layer0(xp, None, pb, pn, None, None, True)
    xs1, _, _ = layer0(xs, sn, sb, sn, state_l0_ssm_fwd.reshape(st_shape),
                       state_l0_ssm_bwd.reshape(st_shape), False)

    qp, new_k, new_v = _inproj1(xp1, mod1, row(norm_pre1), w1_in, row(l1_q_norm), row(l1_k_norm),
                                None, True)
    op = _attention(qp, pn, pb, pn)
    yp = _outproj([op], w1_out, xp1, mod1, row(norm_post1), None)

    (qs,) = _inproj1(xs1, mod1, row(norm_pre1), w1_in, row(l1_q_norm), row(l1_k_norm), sn, False)
    os_ = _attention(qs, sn, sb, min(sn, 512), ctx_k, ctx_v, rope)
    ys = _outproj([os_], w1_out, xs1, mod1, row(norm_post1), sn)

    return (yp.reshape(pb, pn, d), ys.reshape(sb, sn, d),
            new_f.reshape(pb, B_HEADS, B_HEAD_DIM, B_STATE),
            new_b.reshape(pb, B_HEADS, B_HEAD_DIM, B_STATE),
            new_k.reshape(pb, pn, C_KV_HEADS, C_HEAD_DIM),
            new_v.reshape(pb, pn, C_KV_HEADS, C_HEAD_DIM))
```

```python
import functools

import numpy as np
import jax
import jax.numpy as jnp
from jax import lax
from jax.experimental import pallas as pl
from jax.experimental.pallas import tpu as pltpu

F32 = jnp.float32
BF16 = jnp.bfloat16

EPS = 1e-6
LOG2E = 1.4426950408889634
GELU_K = 0.7978845608028654

D_MODEL = 2048
CHUNK = 128
GRID_W = 64
ROPE_THETA = 10000.0
A_WIDTH = 2048
A_GROUPS = 16
B_WIDTH = 2048
B_HEAD_DIM = 64
B_HEADS = 32
B_GROUPS = 8
B_STATE = 128
B_CONV = 5
B_GN = B_GROUPS * B_STATE
B_CONV_CH = B_WIDTH + 2 * B_GN
B_GROUP_CH = B_WIDTH // B_GROUPS
L0_MAIN = 2 * A_WIDTH + A_WIDTH + B_WIDTH + B_CONV_CH
C_HEADS = 16
C_KV_HEADS = 4
C_HEAD_DIM = 128
C_WIDTH = 2048
C_KV_WIDTH = 512
L1_IN = C_WIDTH + 2 * C_KV_WIDTH + C_WIDTH

MOD_ROWS = 16
CTX_ROW = 8
V7X_VMEM_LIMIT = 56 * 1024 * 1024
SUB = 256
GAP = 16


def _silu(x):
    return x * (1.0 / (1.0 + jnp.exp(-x)))


def _silu_tanh(x):
    return x * (0.5 + 0.5 * jnp.tanh(0.5 * x))


def _softplus(x):
    return jnp.maximum(x, 0.0) + jnp.log1p(jnp.exp(-jnp.abs(x)))


def _rms_rows(x):
    return x * lax.rsqrt(jnp.mean(x * x, axis=-1, keepdims=True) + EPS)


def _params(sem, vmem=V7X_VMEM_LIMIT):
    return pltpu.CompilerParams(dimension_semantics=sem, vmem_limit_bytes=vmem)


def _mod_kernel(c_ref, w_ref, b_ref, o_ref):
    s = _silu(c_ref[...]).astype(BF16)
    o_ref[...] = jnp.dot(s, w_ref[...].astype(BF16), preferred_element_type=F32) + b_ref[...]


def _modulation(cond, w, b, tn=1024):
    n = w.shape[1]
    m = pl.pallas_call(
        _mod_kernel,
        out_shape=jax.ShapeDtypeStruct((MOD_ROWS, n), F32),
        grid=(n // tn,),
        in_specs=[pl.BlockSpec((MOD_ROWS, D_MODEL), lambda j: (0, 0)),
                  pl.BlockSpec((D_MODEL, tn), lambda j: (0, j)),
                  pl.BlockSpec((1, tn), lambda j: (0, j))],
        out_specs=pl.BlockSpec((MOD_ROWS, tn), lambda j: (0, j)),
        compiler_params=_params(("arbitrary",)),
        name="modulation",
    )(cond, w, b.reshape(1, n))
    return m.reshape(MOD_ROWS, 3, D_MODEL)


def _mod_row_map(rows_per_cond, tm):
    if rows_per_cond is None:
        return lambda i: CTX_ROW
    return lambda i: (i * tm) // rows_per_cond


def _prenorm(x, mod_ref, g_ref):
    gain = g_ref[...] * (1.0 + mod_ref[0, 1:2, :])
    r = lax.rsqrt(jnp.mean(x * x, axis=-1, keepdims=True) + EPS)
    return (x * r * gain + mod_ref[0, 0:1, :]).astype(BF16)


def _inproj0_kernel(xp_ref, x_ref, xn_ref, mod_ref, g_ref, w_ref, wdt_ref, cw_ref, cb_ref,
                    p_ref, dt_ref, h_ref, hg_ref, acc_ref, *, tm, tn, seq, n_gelu, n_gated):
    i = pl.program_id(0)
    j = pl.program_id(1)
    nseg = max(1, tm // seq)
    seg = tm // nseg
    rows_g = hg_ref.shape[0]

    @pl.when(j == 0)
    def _():
        h = _prenorm(x_ref[...], mod_ref, g_ref)
        h_ref[...] = h
        dt_ref[...] = jnp.dot(h, wdt_ref[...], preferred_element_type=F32)
        zero = jnp.zeros((GAP, D_MODEL), BF16)
        has_prev = (i * tm) % seq != 0
        has_next = ((i + 1) * tm) % seq != 0
        hg_ref[0:GAP, :] = jnp.where(has_prev, _prenorm(xp_ref[...], mod_ref, g_ref), zero)
        for q in range(nseg):
            base = GAP + q * (seg + GAP)
            hg_ref[base:base + seg, :] = h[q * seg:(q + 1) * seg]
            if q < nseg - 1:
                hg_ref[base + seg:base + seg + GAP, :] = zero
        hg_ref[rows_g - GAP:rows_g, :] = jnp.where(has_next, _prenorm(xn_ref[...], mod_ref, g_ref), zero)

    @pl.when(j < n_gelu)
    def _():
        for s in range(tn // SUB):
            sub = slice(s * SUB, (s + 1) * SUB)
            acc = jnp.dot(h_ref[...], w_ref[:, sub], preferred_element_type=F32)
            t = jnp.tanh(acc * (GELU_K + (GELU_K * 0.044715) * (acc * acc)))
            p_ref[:, sub] = (acc * (0.5 + 0.5 * t)).astype(BF16)

    @pl.when((j >= n_gelu) & (j < n_gated))
    def _():
        for s in range(tn // SUB):
            sub = slice(s * SUB, (s + 1) * SUB)
            acc = jnp.dot(h_ref[...], w_ref[:, sub], preferred_element_type=F32)
            p_ref[:, sub] = _silu_tanh(acc).astype(BF16)

    @pl.when(j >= n_gated)
    def _():
        for s in range(tn // SUB):
            sub = slice(s * SUB, (s + 1) * SUB)
            buf = s % 2
            acc_ref[buf] = jnp.dot(hg_ref[...], w_ref[:, sub], preferred_element_type=F32)
            for q in range(nseg):
                base = GAP + q * (seg + GAP)
                out = None
                for k in range(B_CONV):
                    tap = acc_ref[buf, pl.ds(base - B_CONV // 2 + k, seg), :]
                    term = tap * cw_ref[k:k + 1, sub]
                    out = term if out is None else out + term
                p_ref[q * seg:(q + 1) * seg, sub] = _silu_tanh(out + cb_ref[:, sub]).astype(BF16)


def _inproj0(x, mod, g, w_main, w_dt, conv_w, conv_b, rows_per_cond, seq, tm=1024, tn=1024):
    m = x.shape[0]
    row = _mod_row_map(rows_per_cond, tm)
    nseg = max(1, tm // seq)
    rows_g = tm + (nseg + 1) * GAP
    n_gated = (L0_MAIN - B_CONV_CH) // tn
    gb = tm // GAP
    last_gap = m // GAP - 1
    conv_col = lambda j: jnp.maximum(j - n_gated, 0)
    kern = functools.partial(_inproj0_kernel, tm=tm, tn=tn, seq=seq, n_gelu=2 * A_WIDTH // tn, n_gated=n_gated)
    return pl.pallas_call(
        kern,
        out_shape=(jax.ShapeDtypeStruct((m, L0_MAIN), BF16),
                   jax.ShapeDtypeStruct((m, 128), F32)),
        grid=(m // tm, L0_MAIN // tn),
        in_specs=[pl.BlockSpec((GAP, D_MODEL), lambda i, j: (jnp.maximum(i * gb - 1, 0), 0)),
                  pl.BlockSpec((tm, D_MODEL), lambda i, j: (i, 0)),
                  pl.BlockSpec((GAP, D_MODEL), lambda i, j: (jnp.minimum((i + 1) * gb, last_gap), 0)),
                  pl.BlockSpec((1, 3, D_MODEL), lambda i, j: (row(i), 0, 0)),
                  pl.BlockSpec((1, D_MODEL), lambda i, j: (0, 0)),
                  pl.BlockSpec((D_MODEL, tn), lambda i, j: (0, j)),
                  pl.BlockSpec((D_MODEL, 128), lambda i, j: (0, 0)),
                  pl.BlockSpec((8, tn), lambda i, j: (0, conv_col(j))),
                  pl.BlockSpec((1, tn), lambda i, j: (0, conv_col(j)))],
        out_specs=(pl.BlockSpec((tm, tn), lambda i, j: (i, j)),
                   pl.BlockSpec((tm, 128), lambda i, j: (i, 0))),
        scratch_shapes=[pltpu.VMEM((tm, D_MODEL), BF16),
                        pltpu.VMEM((rows_g, D_MODEL), BF16),
                        pltpu.VMEM((2, rows_g, SUB), F32)],
        compiler_params=_params(("parallel", "arbitrary")),
        name="l0_inproj",
    )(x, x, x, mod, g, w_main, w_dt, conv_w, conv_b)


P_U, P_V, P_ZA, P_ZB, P_X = 0, 1, 2, 3, 4
P_BC = 5
P_B = (L0_MAIN - 2 * B_GN) // B_GN


def _mixa_kernel(u_ref, v_ref, z_ref, vg_ref, ws_ref, bs_ref, o_ref, *, n_chunks):
    v = v_ref[...].astype(F32)
    vc = v - jnp.mean(v, axis=-1, keepdims=True)
    vn = vc * lax.rsqrt(jnp.mean(vc * vc, axis=-1, keepdims=True) + EPS) * vg_ref[...]
    vn = vn.astype(BF16)
    gd = A_WIDTH // A_GROUPS
    for c in range(n_chunks):
        rows = slice(c * CHUNK, (c + 1) * CHUNK)
        for g in range(A_GROUPS):
            cols = slice(g * gd, (g + 1) * gd)
            s = jnp.dot(ws_ref[g], vn[rows, cols], preferred_element_type=F32) + bs_ref[g]
            o = u_ref[rows, cols].astype(F32) * s * z_ref[rows, cols].astype(F32)
            o_ref[rows, cols] = o.astype(BF16)


def _split2(x):
    hi = x.astype(BF16)
    lo = (x - hi.astype(F32)).astype(BF16)
    return jnp.concatenate([hi, lo], axis=1)


def _split3(x):
    hi = x.astype(BF16)
    r = x - hi.astype(F32)
    mid = r.astype(BF16)
    lo = (r - mid.astype(F32)).astype(BF16)
    return hi, mid, lo


def _tri_matmul(tri, x):
    hi, mid, lo = _split3(x)
    out = jnp.dot(tri, lo, preferred_element_type=F32)
    out = out + jnp.dot(tri, mid, preferred_element_type=F32)
    return out + jnp.dot(tri, hi, preferred_element_type=F32)


def _expand(x_split, e2):
    return jnp.dot(x_split, e2, preferred_element_type=F32)


def _ssd_small(dt_raw, dtb_ref, alog_ref):
    dt = _softplus(dt_raw + dtb_ref[...])
    da = dt * (-jnp.exp(alog_ref[...]))
    r = lax.broadcasted_iota(jnp.int32, (CHUNK, CHUNK), 0)
    c = lax.broadcasted_iota(jnp.int32, (CHUNK, CHUNK), 1)
    lower = jnp.where(c <= r, 1.0, 0.0).astype(BF16)
    upper = jnp.where(c >= r, 1.0, 0.0).astype(BF16)
    cum = jnp.where(c < B_HEADS, _tri_matmul(lower, da), _tri_matmul(upper, da))
    return dt, cum, r, c


def _init_state(state_ref, h0_ref):
    for g in range(B_GROUPS):
        if h0_ref is not None:
            state_ref[g] = h0_ref[0, g * B_GROUP_CH:(g + 1) * B_GROUP_CH, :].T
        else:
            state_ref[g] = jnp.zeros((B_STATE, B_GROUP_CH), F32)


def _emit_state(hfin_ref, state_ref):
    for g in range(B_GROUPS):
        hfin_ref[0, g * B_GROUP_CH:(g + 1) * B_GROUP_CH, :] = state_ref[g].T


def _ssd_bwd_kernel(*refs, n_chunks, has_init, emit_final):
    it = iter(refs)
    x_ref, b_ref, dtraw_ref, dtb_ref, alog_ref, e2b_ref = (next(it), next(it), next(it), next(it),
                                                          next(it), next(it))
    mixa_in = [next(it) for _ in range(6)]
    h0_ref = next(it) if has_init else None
    hin_ref, sm_ref, smb_ref, a_ref = next(it), next(it), next(it), next(it)
    hfin_ref = next(it) if emit_final else None
    hb_ref = next(it)

    j = pl.program_id(1)
    cps = hin_ref.shape[0]

    @pl.when(j == 0)
    def _():
        _init_state(hb_ref, h0_ref)

    xws, tots = [], []
    for lc in range(cps):
        rows = slice(lc * CHUNK, (lc + 1) * CHUNK)
        dt, cum, _, lane = _ssd_small(dtraw_ref[rows, :], dtb_ref, alog_ref)
        is_bwd = (lane >= B_HEADS) & (lane < 2 * B_HEADS)
        w_b = jnp.where(is_bwd, dt * jnp.exp(jnp.where(is_bwd, cum[0:1, :] - cum, 0.0)), 0.0)
        xws.append((x_ref[rows, :].astype(F32) * _expand(_split2(w_b), e2b_ref[...])).astype(BF16))
        tot_split = _split2(jnp.broadcast_to(jnp.exp(cum[0:1, :]), (GAP, CHUNK)))
        tots.append(_expand(tot_split, e2b_ref[...])[0:1, :])
        cum2 = cum * LOG2E
        ldt2 = jnp.log(dt) * LOG2E
        sm_ref[rows, 0:CHUNK] = cum2
        sm_ref[rows, CHUNK:2 * CHUNK] = (ldt2 - cum2).T
        sm_ref[rows, 2 * CHUNK:3 * CHUNK] = ldt2.T
        is_fwd = lane < B_HEADS
        to_end = jnp.exp(jnp.where(is_fwd, cum[CHUNK - 1:CHUNK, :] - cum, 0.0))
        smb_ref[rows, 0:2 * CHUNK] = _split2(jnp.exp(cum))
        smb_ref[rows, 2 * CHUNK:4 * CHUNK] = _split2(jnp.where(is_fwd, dt * to_end, 0.0))

    for g in range(B_GROUPS):
        cols = slice(g * B_GROUP_CH, (g + 1) * B_GROUP_CH)
        hb_g = hb_ref[g]
        for lc in reversed(range(cps)):
            rows = slice(lc * CHUNK, (lc + 1) * CHUNK)
            hin_ref[lc, g] = hb_g.astype(BF16)
            upd = lax.dot_general(b_ref[rows, g * B_STATE:(g + 1) * B_STATE], xws[lc][:, cols],
                                  (((0,), (0,)), ((), ())), preferred_element_type=F32)
            hb_g = hb_g * tots[lc][:, cols] + upd
        hb_ref[g] = hb_g

    _mixa_kernel(*mixa_in, a_ref, n_chunks=cps)

    if emit_final:
        @pl.when(j == pl.num_programs(1) - 1)
        def _():
            _emit_state(hfin_ref, hb_ref)


def _ssd_bwd(p, dt_raw, dt_bias, a_log, e2b, v_gain, w_s, b_s, h0, n_batch, n_chunks, emit_final, cps=2):
    has_init = h0 is not None
    steps = n_chunks // cps
    rows = cps * CHUNK
    blk = lambda b, j: b * steps + (steps - 1 - j)
    const = lambda b, j: (0, 0)
    in_specs = [
        pl.BlockSpec((rows, B_WIDTH), lambda b, j: (blk(b, j), P_X)),
        pl.BlockSpec((rows, B_GN), lambda b, j: (blk(b, j), P_B)),
        pl.BlockSpec((rows, 128), lambda b, j: (blk(b, j), 0)),
        pl.BlockSpec((1, 128), const),
        pl.BlockSpec((1, 128), const),
        pl.BlockSpec((2 * CHUNK, B_WIDTH), const),
        pl.BlockSpec((rows, A_WIDTH), lambda b, j: (blk(b, j), P_U)),
        pl.BlockSpec((rows, A_WIDTH), lambda b, j: (blk(b, j), P_V)),
        pl.BlockSpec((rows, A_WIDTH), lambda b, j: (blk(b, j), P_ZA)),
        pl.BlockSpec((1, A_WIDTH), const),
        pl.BlockSpec((A_GROUPS, CHUNK, CHUNK), lambda b, j: (0, 0, 0)),
        pl.BlockSpec((A_GROUPS, CHUNK, CHUNK), lambda b, j: (0, 0, 0)),
    ]
    args = [p, p, dt_raw, dt_bias, a_log, e2b, p, p, p, v_gain, w_s, b_s]
    if has_init:
        in_specs.append(pl.BlockSpec((1, B_WIDTH, B_STATE), lambda b, j: (b, 0, 0)))
        args.append(h0)
    m = n_batch * n_chunks * CHUNK
    out_shape = [jax.ShapeDtypeStruct((n_batch * n_chunks, B_GROUPS, B_STATE, B_GROUP_CH), BF16),
                 jax.ShapeDtypeStruct((m, 3 * CHUNK), F32),
                 jax.ShapeDtypeStruct((m, 4 * CHUNK), BF16),
                 jax.ShapeDtypeStruct((m, A_WIDTH), BF16)]
    out_specs = [pl.BlockSpec((cps, B_GROUPS, B_STATE, B_GROUP_CH), lambda b, j: (blk(b, j), 0, 0, 0)),
                 pl.BlockSpec((rows, 3 * CHUNK), lambda b, j: (blk(b, j), 0)),
                 pl.BlockSpec((rows, 4 * CHUNK), lambda b, j: (blk(b, j), 0)),
                 pl.BlockSpec((rows, A_WIDTH), lambda b, j: (blk(b, j), 0))]
    if emit_final:
        out_shape.append(jax.ShapeDtypeStruct((n_batch, B_WIDTH, B_STATE), F32))
        out_specs.append(pl.BlockSpec((1, B_WIDTH, B_STATE), lambda b, j: (b, 0, 0)))
    return pl.pallas_call(
        functools.partial(_ssd_bwd_kernel, n_chunks=n_chunks, has_init=has_init, emit_final=emit_final),
        out_shape=tuple(out_shape),
        grid=(n_batch, steps),
        in_specs=in_specs,
        out_specs=tuple(out_specs),
        scratch_shapes=[pltpu.VMEM((B_GROUPS, B_STATE, B_GROUP_CH), F32)],
        compiler_params=_params(("parallel", "arbitrary")),
        name="l0_ssd_bwd",
    )(*args)


def _ssd_fwd_kernel(*refs, n_chunks, has_init, emit_final):
    it = iter(refs)
    x_ref, bc_ref, sm_ref, smb_ref, z_ref, hin_ref = (next(it), next(it), next(it), next(it),
                                                      next(it), next(it))
    e2f_ref, e2b_ref, dsk_ref, ng_ref = next(it), next(it), next(it), next(it)
    h0_ref = next(it) if has_init else None
    y_ref = next(it)
    hfin_ref = next(it) if emit_final else None
    hf_ref = next(it)

    j = pl.program_id(1)

    @pl.when(j == 0)
    def _():
        _init_state(hf_ref, h0_ref)

    cps = hin_ref.shape[0]
    r = lax.broadcasted_iota(jnp.int32, (CHUNK, CHUNK), 0)
    c = lax.broadcasted_iota(jnp.int32, (CHUNK, CHUNK), 1)
    dsk = dsk_ref[0:1, :] + dsk_ref[1:2, :]
    heads_per_group = B_HEADS // B_GROUPS
    col_head = lax.broadcasted_iota(jnp.int32, (CHUNK, B_GROUP_CH), 1) // B_HEAD_DIM

    for g in range(B_GROUPS):
        cols = slice(g * B_GROUP_CH, (g + 1) * B_GROUP_CH)
        e2f = e2f_ref[:, cols]
        e2b = e2b_ref[:, cols]
        hf_g = hf_ref[g]
        for lc in range(cps):
            rows = slice(lc * CHUNK, (lc + 1) * CHUNK)
            cum2 = sm_ref[rows, 0:CHUNK]
            r2t = sm_ref[rows, CHUNK:2 * CHUNK]
            ldt2t = sm_ref[rows, 2 * CHUNK:3 * CHUNK]
            dec_split = smb_ref[rows, 0:2 * CHUNK]
            dec_f = _expand(dec_split, e2f)
            dec_b = _expand(dec_split, e2b)
            b_g = bc_ref[rows, g * B_STATE:(g + 1) * B_STATE]
            c_g = bc_ref[rows, B_GN + g * B_STATE:B_GN + (g + 1) * B_STATE]
            cb = lax.dot_general(c_g, b_g, (((1,), (1,)), ((), ())), preferred_element_type=F32)
            cb_f = jnp.where(c <= r, cb, 0.0)
            cb_b = jnp.where(c >= r, cb, 0.0)
            y_g = jnp.dot(c_g, hf_g.astype(BF16), preferred_element_type=F32) * dec_f
            y_g = y_g + jnp.dot(c_g, hin_ref[lc, g], preferred_element_type=F32) * dec_b
            x_g = x_ref[rows, cols]
            y_g = y_g + dsk[:, cols] * x_g.astype(F32)
            mats, x_blocks = [], []
            for e in range(heads_per_group):
                h = g * heads_per_group + e
                hb = B_HEADS + h
                arg_f = jnp.minimum(cum2[:, h:h + 1] + r2t[h:h + 1, :], ldt2t[h:h + 1, :])
                arg_b = jnp.minimum(cum2[:, hb:hb + 1] + r2t[hb:hb + 1, :], ldt2t[hb:hb + 1, :])
                mats.append((cb_f * jnp.exp2(arg_f) + cb_b * jnp.exp2(arg_b)).astype(BF16))
                x_blocks.append(jnp.where(col_head == e, x_g, jnp.zeros_like(x_g)))
            y_g = y_g + jnp.dot(jnp.concatenate(mats, axis=1), jnp.concatenate(x_blocks, axis=0),
                                preferred_element_type=F32)
            xw = (x_g.astype(F32) * _expand(smb_ref[rows, 2 * CHUNK:4 * CHUNK], e2f)).astype(BF16)
            upd = lax.dot_general(b_g, xw, (((0,), (0,)), ((), ())), preferred_element_type=F32)
            hf_g = hf_g * dec_f[CHUNK - 1:CHUNK, :] + upd
            yz = y_g * z_ref[rows, cols].astype(F32)
            y_ref[rows, cols] = (_rms_rows(yz) * ng_ref[:, cols]).astype(BF16)
        hf_ref[g] = hf_g

    if emit_final:
        @pl.when(j == pl.num_programs(1) - 1)
        def _():
            _emit_state(hfin_ref, hf_ref)


def _ssd_fwd(p, small, small_bf, hin, e2f, e2b, d_skip, norm_g, h0, n_batch, n_chunks, emit_final, cps=2):
    m = p.shape[0]
    has_init = h0 is not None
    steps = n_chunks // cps
    rows = cps * CHUNK
    blk = lambda b, j: b * steps + j
    const = lambda b, j: (0, 0)
    in_specs = [
        pl.BlockSpec((rows, B_WIDTH), lambda b, j: (blk(b, j), P_X)),
        pl.BlockSpec((rows, 2 * B_GN), lambda b, j: (blk(b, j), P_BC)),
        pl.BlockSpec((rows, 3 * CHUNK), lambda b, j: (blk(b, j), 0)),
        pl.BlockSpec((rows, 4 * CHUNK), lambda b, j: (blk(b, j), 0)),
        pl.BlockSpec((rows, B_WIDTH), lambda b, j: (blk(b, j), P_ZB)),
        pl.BlockSpec((cps, B_GROUPS, B_STATE, B_GROUP_CH), lambda b, j: (blk(b, j), 0, 0, 0)),
        pl.BlockSpec((2 * CHUNK, B_WIDTH), const),
        pl.BlockSpec((2 * CHUNK, B_WIDTH), const),
        pl.BlockSpec((2, B_WIDTH), const),
        pl.BlockSpec((1, B_WIDTH), const),
    ]
    args = [p, p, small, small_bf, p, hin, e2f, e2b, d_skip, norm_g]
    if has_init:
        in_specs.append(pl.BlockSpec((1, B_WIDTH, B_STATE), lambda b, j: (b, 0, 0)))
        args.append(h0)
    out_shape = [jax.ShapeDtypeStruct((m, B_WIDTH), BF16)]
    out_specs = [pl.BlockSpec((rows, B_WIDTH), lambda b, j: (blk(b, j), 0))]
    if emit_final:
        out_shape.append(jax.ShapeDtypeStruct((n_batch, B_WIDTH, B_STATE), F32))
        out_specs.append(pl.BlockSpec((1, B_WIDTH, B_STATE), lambda b, j: (b, 0, 0)))
    return pl.pallas_call(
        functools.partial(_ssd_fwd_kernel, n_chunks=n_chunks, has_init=has_init, emit_final=emit_final),
        out_shape=tuple(out_shape),
        grid=(n_batch, steps),
        in_specs=in_specs,
        out_specs=tuple(out_specs),
        scratch_shapes=[pltpu.VMEM((B_GROUPS, B_STATE, B_GROUP_CH), F32)],
        compiler_params=_params(("parallel", "arbitrary")),
        name="l0_ssd_fwd",
    )(*args)


def _outproj_kernel(*refs, n_in):
    acts, ws = refs[:n_in], refs[n_in:2 * n_in]
    x_ref, mod_ref, g_ref, o_ref = refs[2 * n_in:]
    acc = jnp.dot(acts[0][...], ws[0][...], preferred_element_type=F32)
    for a_ref, w_ref in zip(acts[1:], ws[1:]):
        acc = acc + jnp.dot(a_ref[...], w_ref[...], preferred_element_type=F32)
    gain = mod_ref[0, 2:3, :] * g_ref[...]
    r = lax.rsqrt(jnp.mean(acc * acc, axis=-1, keepdims=True) + EPS)
    o_ref[...] = x_ref[...] + acc * r * gain


def _outproj(acts, w, x, mod, g, rows_per_cond, tm=512):
    m = x.shape[0]
    n_in = len(acts)
    kdim = acts[0].shape[1]
    row = _mod_row_map(rows_per_cond, tm)
    in_specs = [pl.BlockSpec((tm, kdim), lambda i: (i, 0)) for _ in acts]
    in_specs += [pl.BlockSpec((kdim, D_MODEL), functools.partial(lambda i, k: (k, 0), k=k),
                              pipeline_mode=pl.Buffered(1)) for k in range(n_in)]
    in_specs += [pl.BlockSpec((tm, D_MODEL), lambda i: (i, 0)),
                 pl.BlockSpec((1, 3, D_MODEL), lambda i: (row(i), 0, 0)),
                 pl.BlockSpec((1, D_MODEL), lambda i: (0, 0))]
    return pl.pallas_call(
        functools.partial(_outproj_kernel, n_in=n_in),
        out_shape=jax.ShapeDtypeStruct((m, D_MODEL), F32),
        grid=(m // tm,),
        in_specs=in_specs,
        out_specs=pl.BlockSpec((tm, D_MODEL), lambda i: (i, 0)),
        compiler_params=_params(("parallel",)),
        name="outproj",
    )(*acts, *([w] * n_in), x, mod, g)


Q_SCALE = C_HEAD_DIM ** -0.5 * LOG2E


def _inproj1_kernel(*refs, tn, emit_kv):
    it = iter(refs)
    x_ref, mod_ref, g_ref, w_ref, qn_ref, kn_ref = next(it), next(it), next(it), next(it), next(it), next(it)
    p_ref = next(it)
    k_ref = next(it) if emit_kv else None
    v_ref = next(it) if emit_kv else None
    h_ref = next(it)

    j = pl.program_id(1)
    nq = C_WIDTH // tn

    @pl.when(j == 0)
    def _():
        h_ref[...] = _prenorm(x_ref[...], mod_ref, g_ref)

    def sub_dot(s):
        return jnp.dot(h_ref[...], w_ref[:, s * SUB:(s + 1) * SUB], preferred_element_type=F32)

    @pl.when(j < nq)
    def _():
        for s in range(tn // SUB):
            acc = sub_dot(s)
            for hh in range(SUB // C_HEAD_DIM):
                q = _rms_rows(acc[:, hh * C_HEAD_DIM:(hh + 1) * C_HEAD_DIM]) * (qn_ref[...] * Q_SCALE)
                col = s * SUB + hh * C_HEAD_DIM
                p_ref[:, col:col + C_HEAD_DIM] = q.astype(BF16)

    @pl.when(j == nq)
    def _():
        for s in range(tn // SUB):
            acc = sub_dot(s)
            if s * SUB < C_KV_WIDTH:
                for hh in range(SUB // C_HEAD_DIM):
                    k = _rms_rows(acc[:, hh * C_HEAD_DIM:(hh + 1) * C_HEAD_DIM]) * kn_ref[...]
                    col = s * SUB + hh * C_HEAD_DIM
                    p_ref[:, col:col + C_HEAD_DIM] = k.astype(BF16)
                    if emit_kv:
                        k_ref[:, col:col + C_HEAD_DIM] = k
            else:
                p_ref[:, s * SUB:(s + 1) * SUB] = acc.astype(BF16)
                if emit_kv:
                    v_ref[:, s * SUB - C_KV_WIDTH:(s + 1) * SUB - C_KV_WIDTH] = acc

    @pl.when(j > nq)
    def _():
        for s in range(tn // SUB):
            p_ref[:, s * SUB:(s + 1) * SUB] = _silu_tanh(sub_dot(s)).astype(BF16)


def _inproj1(x, mod, g, w, q_norm, k_norm, rows_per_cond, emit_kv, tm=1024, tn=1024):
    m = x.shape[0]
    assert tn == 2 * C_KV_WIDTH
    row = _mod_row_map(rows_per_cond, tm)
    nq = C_WIDTH // tn
    nblk = L1_IN // tn
    out_col = lambda j: jnp.where(j < nq, j, jnp.where(j == nq, nblk - 1, j - 1))
    in_specs = [pl.BlockSpec((tm, D_MODEL), lambda i, j: (i, 0)),
                pl.BlockSpec((1, 3, D_MODEL), lambda i, j: (row(i), 0, 0)),
                pl.BlockSpec((1, D_MODEL), lambda i, j: (0, 0)),
                pl.BlockSpec((D_MODEL, tn), lambda i, j: (0, j)),
                pl.BlockSpec((1, C_HEAD_DIM), lambda i, j: (0, 0)),
                pl.BlockSpec((1, C_HEAD_DIM), lambda i, j: (0, 0))]
    out_shape = [jax.ShapeDtypeStruct((m, L1_IN), BF16)]
    out_specs = [pl.BlockSpec((tm, tn), lambda i, j: (i, out_col(j)))]
    if emit_kv:
        out_shape += [jax.ShapeDtypeStruct((m, C_KV_WIDTH), F32)] * 2
        out_specs += [pl.BlockSpec((tm, C_KV_WIDTH), lambda i, j: (i, 0))] * 2
    return pl.pallas_call(
        functools.partial(_inproj1_kernel, tn=tn, emit_kv=emit_kv),
        out_shape=tuple(out_shape),
        grid=(m // tm, nblk),
        in_specs=in_specs,
        out_specs=tuple(out_specs),
        scratch_shapes=[pltpu.VMEM((tm, D_MODEL), BF16)],
        compiler_params=_params(("parallel", "arbitrary")),
        name="l1_inproj",
    )(x, mod, g, w, q_norm, k_norm)


def _rope(x_bf16, cos, sin_signed):
    x = x_bf16.astype(F32)
    lane = lax.broadcasted_iota(jnp.int32, x.shape, 1)
    swapped = jnp.where((lane // 32) % 2 == 0, pltpu.roll(x, C_HEAD_DIM - 32, axis=1), pltpu.roll(x, 32, axis=1))
    return (x * cos + swapped * sin_signed).astype(BF16)


def _attn_kernel(*refs, has_ctx, use_rope):
    it = iter(refs)
    q_ref, z_ref, kn_ref, vn_ref = next(it), next(it), next(it), next(it)
    kc_ref = next(it) if has_ctx else None
    vc_ref = next(it) if has_ctx else None
    if use_rope:
        cosq_ref, sinq_ref, cosk_ref, sink_ref = next(it), next(it), next(it), next(it)
    o_ref, kk_ref, vv_ref = next(it), next(it), next(it)
    past = kc_ref.shape[1] if has_ctx else 0
    n_new = kn_ref.shape[0]
    rep = C_HEADS // C_KV_HEADS
    nt = (((1,), (1,)), ((), ()))

    @pl.when(pl.program_id(1) == 0)
    def _():
        for kv in range(C_KV_HEADS):
            kcols = slice(kv * C_HEAD_DIM, (kv + 1) * C_HEAD_DIM)
            if has_ctx:
                kk_ref[kv, 0:past, :] = kc_ref[0, :, kcols]
                vv_ref[kv, 0:past, 0:C_HEAD_DIM] = vc_ref[0, :, kcols]
            k_new = kn_ref[:, kcols]
            kk_ref[kv, past:past + n_new, :] = _rope(k_new, cosk_ref[...], sink_ref[...]) if use_rope else k_new
            vv_ref[kv, past:past + n_new, 0:C_HEAD_DIM] = vn_ref[:, kcols]
            vv_ref[kv, :, C_HEAD_DIM:2 * C_HEAD_DIM] = jnp.ones((past + n_new, C_HEAD_DIM), BF16)

    for kv in range(C_KV_HEADS):
        for e in range(rep):
            cols = slice((kv * rep + e) * C_HEAD_DIM, (kv * rep + e + 1) * C_HEAD_DIM)
            q = _rope(q_ref[:, cols], cosq_ref[...], sinq_ref[...]) if use_rope else q_ref[:, cols]
            s = lax.dot_general(q, kk_ref[kv], nt, preferred_element_type=F32)
            mx = jnp.max(s, axis=-1, keepdims=True)
            o = jnp.dot(jnp.exp2(s - mx).astype(BF16), vv_ref[kv], preferred_element_type=F32)
            out = o[:, :C_HEAD_DIM] * (1.0 / o[:, C_HEAD_DIM:]) * z_ref[:, cols].astype(F32)
            o_ref[:, cols] = out.astype(BF16)


def _attention(qkvz, seq_len, n_batch, tq, ctx_k=None, ctx_v=None, rope=None):
    m = qkvz.shape[0]
    qb = seq_len // tq
    kcol = (2 * C_WIDTH) // C_KV_WIDTH
    has_ctx = ctx_k is not None
    use_rope = rope is not None
    past = ctx_k.shape[1] if has_ctx else 0
    in_specs = [pl.BlockSpec((tq, C_WIDTH), lambda b, i: (b * qb + i, 0)),
                pl.BlockSpec((tq, C_WIDTH), lambda b, i: (b * qb + i, 1)),
                pl.BlockSpec((seq_len, C_KV_WIDTH), lambda b, i: (b, kcol)),
                pl.BlockSpec((seq_len, C_KV_WIDTH), lambda b, i: (b, kcol + 1))]
    args = [qkvz, qkvz, qkvz, qkvz]
    if has_ctx:
        in_specs += [pl.BlockSpec((1, past, C_KV_WIDTH), lambda b, i: (b, 0, 0))] * 2
        args += [ctx_k, ctx_v]
    if use_rope:
        in_specs += [pl.BlockSpec((tq, C_HEAD_DIM), lambda b, i: (i, 0))] * 2
        in_specs += [pl.BlockSpec((seq_len, C_HEAD_DIM), lambda b, i: (0, 0))] * 2
        args += [*rope, *rope]
    return pl.pallas_call(
        functools.partial(_attn_kernel, has_ctx=has_ctx, use_rope=use_rope),
        out_shape=jax.ShapeDtypeStruct((m, C_WIDTH), BF16),
        grid=(n_batch, qb),
        in_specs=in_specs,
        out_specs=pl.BlockSpec((tq, C_WIDTH), lambda b, i: (b * qb + i, 0)),
        scratch_shapes=[pltpu.VMEM((C_KV_HEADS, past + seq_len, C_HEAD_DIM), BF16),
                        pltpu.VMEM((C_KV_HEADS, past + seq_len, 2 * C_HEAD_DIM), BF16)],
        compiler_params=_params(("parallel", "arbitrary")),
        name="l1_attention",
    )(*args)


def _expand_matrix(first_row):
    e = np.zeros((2 * CHUNK, B_WIDTH), np.float32)
    ch = np.arange(B_WIDTH)
    e[first_row + ch // B_HEAD_DIM, ch] = 1.0
    e[CHUNK + first_row + ch // B_HEAD_DIM, ch] = 1.0
    return jnp.asarray(e, BF16)


def _rope_tables(n):
    t = np.arange(n)
    pos = np.stack([t // GRID_W, t % GRID_W], axis=1).astype(np.float64)
    half = C_HEAD_DIM // 2
    inv = ROPE_THETA ** (-np.arange(0, half, 2, dtype=np.float64) / half)
    ang = pos[:, :, None] * inv[None, None, :]
    cos = np.concatenate([np.cos(ang), np.cos(ang)], axis=-1).reshape(n, C_HEAD_DIM)
    sin = np.concatenate([-np.sin(ang), np.sin(ang)], axis=-1).reshape(n, C_HEAD_DIM)
    return jnp.asarray(cos, F32), jnp.asarray(sin, F32)


def _pad_lanes(x, width=128):
    return jnp.pad(x, ((0, 0), (0, width - x.shape[1])))


def kernel(x_prompt, x_sample, state_l0_ssm_fwd, state_l0_ssm_bwd, cache_l1_k, cache_l1_v, c, c_ctx, mod_w0, mod_b0, norm_pre0, norm_post0, l0_w_in, l0_v_gain, l0_w_s, l0_b_s, l0_conv_w, l0_conv_b, l0_dt_bias, l0_a_log, l0_d_skip, l0_ssm_norm, l0_w_out, mod_w1, mod_b1, norm_pre1, norm_post1, l1_w_in, l1_q_norm, l1_k_norm, l1_w_out):
    pb, pn, d = x_prompt.shape
    sb, sn, _ = x_sample.shape
    xp = x_prompt.reshape(pb * pn, d)
    xs = x_sample.reshape(sb * sn, d)
    row = lambda v: v.reshape(1, -1)

    cond = jnp.zeros((MOD_ROWS, d), F32).at[:sb].set(c).at[CTX_ROW].set(c_ctx)
    w0_main = l0_w_in.astype(BF16)
    w0_dt = _pad_lanes(l0_w_in[:, L0_MAIN:]).astype(BF16)
    w0_out = l0_w_out.astype(BF16)
    ws = l0_w_s.astype(BF16)
    bs = jnp.broadcast_to(l0_b_s[:, :, None], (A_GROUPS, CHUNK, CHUNK))
    conv_w = jnp.pad(l0_conv_w, ((0, 8 - B_CONV), (0, 0)))
    dt_bias = _pad_lanes(l0_dt_bias.reshape(1, 2 * B_HEADS))
    a_log = _pad_lanes(l0_a_log.reshape(1, 2 * B_HEADS))
    d_skip = jnp.repeat(l0_d_skip, B_HEAD_DIM, axis=1)
    e2f, e2b = _expand_matrix(0), _expand_matrix(B_HEADS)
    w1_in = l1_w_in.astype(BF16)
    w1_out = l1_w_out.astype(BF16)
    ctx_k = cache_l1_k.reshape(sb, -1, C_KV_WIDTH).astype(BF16)
    ctx_v = cache_l1_v.reshape(sb, -1, C_KV_WIDTH).astype(BF16)
    rope = _rope_tables(sn)

    mod0 = _modulation(cond, mod_w0, mod_b0)
    mod1 = _modulation(cond, mod_w1, mod_b1)

    def layer0(x, rows_per_cond, n_batch, seq, h0f, h0b, emit_final):
        n_chunks = seq // CHUNK
        p, dt_raw = _inproj0(x, mod0, row(norm_pre0), w0_main, w0_dt, conv_w, row(l0_conv_b),
                             rows_per_cond, seq)
        bwd = _ssd_bwd(p, dt_raw, dt_bias, a_log, e2b, row(l0_v_gain), ws, bs, h0b,
                       n_batch, n_chunks, emit_final, cps=min(4, n_chunks))
        fwd = _ssd_fwd(p, bwd[1], bwd[2], bwd[0], e2f, e2b, d_skip, row(l0_ssm_norm),
                       h0f, n_batch, n_chunks, emit_final, cps=min(4, n_chunks))
        x1 = _outproj([bwd[3], fwd[0]], w0_out, x, mod0, row(norm_post0), rows_per_cond)
        return x1, (fwd[1] if emit_final else None), (bwd[4] if emit_final else None)

    st_shape = (sb, B_WIDTH, B_STATE)
    xp1, new_f, new_b = layer0(xp, None, pb, pn, None, None, True)
    xs1, _, _ = layer0(xs, sn, sb, sn, state_l0_ssm_fwd.reshape(st_shape),
                       state_l0_ssm_bwd.reshape(st_shape), False)

    qp, new_k, new_v = _inproj1(xp1, mod1, row(norm_pre1), w1_in, row(l1_q_norm), row(l1_k_norm),
                                None, True)
    op = _attention(qp, pn, pb, pn)
    yp = _outproj([op], w1_out, xp1, mod1, row(norm_post1), None)

    (qs,) = _inproj1(xs1, mod1, row(norm_pre1), w1_in, row(l1_q_norm), row(l1_k_norm), sn, False)
    os_ = _attention(qs, sn, sb, min(sn, 512), ctx_k, ctx_v, rope)
    ys = _outproj([os_], w1_out, xs1, mod1, row(norm_post1), sn)

    return (yp.reshape(pb, pn, d), ys.reshape(sb, sn, d),
            new_f.reshape(pb, B_HEADS, B_HEAD_DIM, B_STATE),
            new_b.reshape(pb, B_HEADS, B_HEAD_DIM, B_STATE),
            new_k.reshape(pb, pn, C_KV_HEADS, C_HEAD_DIM),
            new_v.reshape(pb, pn, C_KV_HEADS, C_HEAD_DIM))
```

```python
import functools

import numpy as np
import jax
import jax.numpy as jnp
from jax import lax
from jax.experimental import pallas as pl
from jax.experimental.pallas import tpu as pltpu

F32 = jnp.float32
BF16 = jnp.bfloat16

EPS = 1e-6
LOG2E = 1.4426950408889634
GELU_K = 0.7978845608028654

D_MODEL = 2048
CHUNK = 128
GRID_W = 64
ROPE_THETA = 10000.0
A_WIDTH = 2048
A_GROUPS = 16
B_WIDTH = 2048
B_HEAD_DIM = 64
B_HEADS = 32
B_GROUPS = 8
B_STATE = 128
B_CONV = 5
B_GN = B_GROUPS * B_STATE
B_CONV_CH = B_WIDTH + 2 * B_GN
B_GROUP_CH = B_WIDTH // B_GROUPS
L0_MAIN = 2 * A_WIDTH + A_WIDTH + B_WIDTH + B_CONV_CH
C_HEADS = 16
C_KV_HEADS = 4
C_HEAD_DIM = 128
C_WIDTH = 2048
C_KV_WIDTH = 512
L1_IN = C_WIDTH + 2 * C_KV_WIDTH + C_WIDTH

MOD_ROWS = 16
CTX_ROW = 8
V7X_VMEM_LIMIT = 56 * 1024 * 1024
SUB = 256
GAP = 16


def _silu(x):
    return x * (1.0 / (1.0 + jnp.exp(-x)))


def _silu_tanh(x):
    h = 0.5 * x
    return h + h * jnp.tanh(h)


def _softplus(x):
    return jnp.maximum(x, 0.0) + jnp.log1p(jnp.exp(-jnp.abs(x)))


def _rms_rows(x):
    return x * lax.rsqrt(jnp.mean(x * x, axis=-1, keepdims=True) + EPS)


def _params(sem, vmem=V7X_VMEM_LIMIT):
    return pltpu.CompilerParams(dimension_semantics=sem, vmem_limit_bytes=vmem)


def _mod_kernel(c_ref, w_ref, b_ref, o_ref):
    s = _silu(c_ref[...]).astype(BF16)
    o_ref[...] = jnp.dot(s, w_ref[...].astype(BF16), preferred_element_type=F32) + b_ref[...]


def _modulation(cond, w, b, tn=1024):
    n = w.shape[1]
    m = pl.pallas_call(
        _mod_kernel,
        out_shape=jax.ShapeDtypeStruct((MOD_ROWS, n), F32),
        grid=(n // tn,),
        in_specs=[pl.BlockSpec((MOD_ROWS, D_MODEL), lambda j: (0, 0)),
                  pl.BlockSpec((D_MODEL, tn), lambda j: (0, j)),
                  pl.BlockSpec((1, tn), lambda j: (0, j))],
        out_specs=pl.BlockSpec((MOD_ROWS, tn), lambda j: (0, j)),
        compiler_params=_params(("arbitrary",)),
        name="modulation",
    )(cond, w, b.reshape(1, n))
    return m.reshape(MOD_ROWS, 3, D_MODEL)


def _mod_row_map(rows_per_cond, tm):
    if rows_per_cond is None:
        return lambda i: CTX_ROW
    return lambda i: (i * tm) // rows_per_cond


def _prenorm(x, mod_ref, g_ref):
    gain = g_ref[...] * (1.0 + mod_ref[0, 1:2, :])
    r = lax.rsqrt(jnp.mean(x * x, axis=-1, keepdims=True) + EPS)
    return (x * r * gain + mod_ref[0, 0:1, :]).astype(BF16)


def _inproj0_kernel(xp_ref, x_ref, xn_ref, mod_ref, g_ref, w_ref, wdt_ref, cw_ref, cb_ref,
                    p_ref, dt_ref, h_ref, hg_ref, acc_ref, *, tm, tn, seq, n_gelu, n_gated):
    i = pl.program_id(0)
    j = pl.program_id(1)
    nseg = max(1, tm // seq)
    seg = tm // nseg
    rows_g = hg_ref.shape[0]

    @pl.when(j == 0)
    def _():
        h = _prenorm(x_ref[...], mod_ref, g_ref)
        h_ref[...] = h
        dt_ref[...] = jnp.dot(h, wdt_ref[...], preferred_element_type=F32)
        zero = jnp.zeros((GAP, D_MODEL), BF16)
        has_prev = (i * tm) % seq != 0
        has_next = ((i + 1) * tm) % seq != 0
        hg_ref[0:GAP, :] = jnp.where(has_prev, _prenorm(xp_ref[...], mod_ref, g_ref), zero)
        for q in range(nseg):
            base = GAP + q * (seg + GAP)
            hg_ref[base:base + seg, :] = h[q * seg:(q + 1) * seg]
            if q < nseg - 1:
                hg_ref[base + seg:base + seg + GAP, :] = zero
        hg_ref[rows_g - GAP:rows_g, :] = jnp.where(has_next, _prenorm(xn_ref[...], mod_ref, g_ref), zero)

    @pl.when(j < n_gelu)
    def _():
        for s in range(tn // SUB):
            sub = slice(s * SUB, (s + 1) * SUB)
            acc = jnp.dot(h_ref[...], w_ref[:, sub], preferred_element_type=F32)
            t = jnp.tanh(acc * (GELU_K + (GELU_K * 0.044715) * (acc * acc)))
            half = 0.5 * acc
            p_ref[:, sub] = (half + half * t).astype(BF16)

    @pl.when((j >= n_gelu) & (j < n_gated))
    def _():
        for s in range(tn // SUB):
            sub = slice(s * SUB, (s + 1) * SUB)
            acc = jnp.dot(h_ref[...], w_ref[:, sub], preferred_element_type=F32)
            p_ref[:, sub] = _silu_tanh(acc).astype(BF16)

    @pl.when(j >= n_gated)
    def _():
        for s in range(tn // SUB):
            sub = slice(s * SUB, (s + 1) * SUB)
            buf = s % 2
            acc_ref[buf] = jnp.dot(hg_ref[...], w_ref[:, sub], preferred_element_type=F32)
            for q in range(nseg):
                base = GAP + q * (seg + GAP)
                out = None
                for k in range(B_CONV):
                    tap = acc_ref[buf, pl.ds(base - B_CONV // 2 + k, seg), :]
                    term = tap * cw_ref[k:k + 1, sub]
                    out = term if out is None else out + term
                p_ref[q * seg:(q + 1) * seg, sub] = _silu_tanh(out + cb_ref[:, sub]).astype(BF16)


def _inproj0(x, mod, g, w_main, w_dt, conv_w, conv_b, rows_per_cond, seq, tm=1024, tn=1024):
    m = x.shape[0]
    row = _mod_row_map(rows_per_cond, tm)
    nseg = max(1, tm // seq)
    rows_g = tm + (nseg + 1) * GAP
    n_gated = (L0_MAIN - B_CONV_CH) // tn
    gb = tm // GAP
    last_gap = m // GAP - 1
    conv_col = lambda j: jnp.maximum(j - n_gated, 0)
    kern = functools.partial(_inproj0_kernel, tm=tm, tn=tn, seq=seq, n_gelu=2 * A_WIDTH // tn, n_gated=n_gated)
    return pl.pallas_call(
        kern,
        out_shape=(jax.ShapeDtypeStruct((m, L0_MAIN), BF16),
                   jax.ShapeDtypeStruct((m, 128), F32)),
        grid=(m // tm, L0_MAIN // tn),
        in_specs=[pl.BlockSpec((GAP, D_MODEL), lambda i, j: (jnp.maximum(i * gb - 1, 0), 0)),
                  pl.BlockSpec((tm, D_MODEL), lambda i, j: (i, 0)),
                  pl.BlockSpec((GAP, D_MODEL), lambda i, j: (jnp.minimum((i + 1) * gb, last_gap), 0)),
                  pl.BlockSpec((1, 3, D_MODEL), lambda i, j: (row(i), 0, 0)),
                  pl.BlockSpec((1, D_MODEL), lambda i, j: (0, 0)),
                  pl.BlockSpec((D_MODEL, tn), lambda i, j: (0, j)),
                  pl.BlockSpec((D_MODEL, 128), lambda i, j: (0, 0)),
                  pl.BlockSpec((8, tn), lambda i, j: (0, conv_col(j))),
                  pl.BlockSpec((1, tn), lambda i, j: (0, conv_col(j)))],
        out_specs=(pl.BlockSpec((tm, tn), lambda i, j: (i, j)),
                   pl.BlockSpec((tm, 128), lambda i, j: (i, 0))),
        scratch_shapes=[pltpu.VMEM((tm, D_MODEL), BF16),
                        pltpu.VMEM((rows_g, D_MODEL), BF16),
                        pltpu.VMEM((2, rows_g, SUB), F32)],
        compiler_params=_params(("parallel", "arbitrary")),
        name="l0_inproj",
    )(x, x, x, mod, g, w_main, w_dt, conv_w, conv_b)


P_U, P_V, P_ZA, P_ZB, P_X = 0, 1, 2, 3, 4
P_BC = 5
P_B = (L0_MAIN - 2 * B_GN) // B_GN


def _mixa_kernel(u_ref, v_ref, z_ref, vg_ref, ws_ref, bs_ref, o_ref, *, n_chunks):
    v = v_ref[...].astype(F32)
    vc = v - jnp.mean(v, axis=-1, keepdims=True)
    vn = vc * lax.rsqrt(jnp.mean(vc * vc, axis=-1, keepdims=True) + EPS) * vg_ref[...]
    vn = vn.astype(BF16)
    gd = A_WIDTH // A_GROUPS
    for c in range(n_chunks):
        rows = slice(c * CHUNK, (c + 1) * CHUNK)
        for g in range(A_GROUPS):
            cols = slice(g * gd, (g + 1) * gd)
            s = jnp.dot(ws_ref[g], vn[rows, cols], preferred_element_type=F32) + bs_ref[g]
            o = u_ref[rows, cols].astype(F32) * s * z_ref[rows, cols].astype(F32)
            o_ref[rows, cols] = o.astype(BF16)


def _split2(x):
    hi = x.astype(BF16)
    lo = (x - hi.astype(F32)).astype(BF16)
    return jnp.concatenate([hi, lo], axis=1)


def _split3(x):
    hi = x.astype(BF16)
    r = x - hi.astype(F32)
    mid = r.astype(BF16)
    lo = (r - mid.astype(F32)).astype(BF16)
    return hi, mid, lo


def _tri_matmul(tri, x):
    hi, mid, lo = _split3(x)
    out = jnp.dot(tri, lo, preferred_element_type=F32)
    out = out + jnp.dot(tri, mid, preferred_element_type=F32)
    return out + jnp.dot(tri, hi, preferred_element_type=F32)


def _expand(x_split, e2):
    return jnp.dot(x_split, e2, preferred_element_type=F32)


def _ssd_small(dt_raw, dtb_ref, alog_ref):
    dt = _softplus(dt_raw + dtb_ref[...])
    da = dt * (-jnp.exp(alog_ref[...]))
    r = lax.broadcasted_iota(jnp.int32, (CHUNK, CHUNK), 0)
    c = lax.broadcasted_iota(jnp.int32, (CHUNK, CHUNK), 1)
    lower = jnp.where(c <= r, 1.0, 0.0).astype(BF16)
    upper = jnp.where(c >= r, 1.0, 0.0).astype(BF16)
    cum = jnp.where(c < B_HEADS, _tri_matmul(lower, da), _tri_matmul(upper, da))
    return dt, cum, r, c


def _init_state(state_ref, h0_ref):
    for g in range(B_GROUPS):
        if h0_ref is not None:
            state_ref[g] = h0_ref[0, g * B_GROUP_CH:(g + 1) * B_GROUP_CH, :].T
        else:
            state_ref[g] = jnp.zeros((B_STATE, B_GROUP_CH), F32)


def _emit_state(hfin_ref, state_ref):
    for g in range(B_GROUPS):
        hfin_ref[0, g * B_GROUP_CH:(g + 1) * B_GROUP_CH, :] = state_ref[g].T


def _ssd_bwd_kernel(*refs, n_chunks, has_init, emit_final):
    it = iter(refs)
    x_ref, b_ref, dtraw_ref, dtb_ref, alog_ref, e2b_ref = (next(it), next(it), next(it), next(it),
                                                          next(it), next(it))
    mixa_in = [next(it) for _ in range(6)]
    h0_ref = next(it) if has_init else None
    hin_ref, sm_ref, smb_ref, a_ref = next(it), next(it), next(it), next(it)
    hfin_ref = next(it) if emit_final else None
    hb_ref = next(it)

    j = pl.program_id(1)
    cps = hin_ref.shape[0]

    @pl.when(j == 0)
    def _():
        _init_state(hb_ref, h0_ref)

    xws, tots = [], []
    for lc in range(cps):
        rows = slice(lc * CHUNK, (lc + 1) * CHUNK)
        dt, cum, _, lane = _ssd_small(dtraw_ref[rows, :], dtb_ref, alog_ref)
        is_bwd = (lane >= B_HEADS) & (lane < 2 * B_HEADS)
        w_b = jnp.where(is_bwd, dt * jnp.exp(jnp.where(is_bwd, cum[0:1, :] - cum, 0.0)), 0.0)
        xws.append((x_ref[rows, :].astype(F32) * _expand(_split2(w_b), e2b_ref[...])).astype(BF16))
        tot_split = _split2(jnp.broadcast_to(jnp.exp(cum[0:1, :]), (GAP, CHUNK)))
        tots.append(_expand(tot_split, e2b_ref[...])[0:1, :])
        cum2 = cum * LOG2E
        ldt2 = jnp.log(dt) * LOG2E
        sm_ref[rows, 0:CHUNK] = cum2
        sm_ref[rows, CHUNK:2 * CHUNK] = (ldt2 - cum2).T
        sm_ref[rows, 2 * CHUNK:3 * CHUNK] = ldt2.T
        is_fwd = lane < B_HEADS
        to_end = jnp.exp(jnp.where(is_fwd, cum[CHUNK - 1:CHUNK, :] - cum, 0.0))
        smb_ref[rows, 0:2 * CHUNK] = _split2(jnp.exp(cum))
        smb_ref[rows, 2 * CHUNK:4 * CHUNK] = _split2(jnp.where(is_fwd, dt * to_end, 0.0))

    for g in range(B_GROUPS):
        cols = slice(g * B_GROUP_CH, (g + 1) * B_GROUP_CH)
        hb_g = hb_ref[g]
        for lc in reversed(range(cps)):
            rows = slice(lc * CHUNK, (lc + 1) * CHUNK)
            hin_ref[lc, g] = hb_g.astype(BF16)
            upd = lax.dot_general(b_ref[rows, g * B_STATE:(g + 1) * B_STATE], xws[lc][:, cols],
                                  (((0,), (0,)), ((), ())), preferred_element_type=F32)
            hb_g = hb_g * tots[lc][:, cols] + upd
        hb_ref[g] = hb_g

    _mixa_kernel(*mixa_in, a_ref, n_chunks=cps)

    if emit_final:
        @pl.when(j == pl.num_programs(1) - 1)
        def _():
            _emit_state(hfin_ref, hb_ref)


def _ssd_bwd(p, dt_raw, dt_bias, a_log, e2b, v_gain, w_s, b_s, h0, n_batch, n_chunks, emit_final, cps=2):
    has_init = h0 is not None
    steps = n_chunks // cps
    rows = cps * CHUNK
    blk = lambda b, j: b * steps + (steps - 1 - j)
    const = lambda b, j: (0, 0)
    in_specs = [
        pl.BlockSpec((rows, B_WIDTH), lambda b, j: (blk(b, j), P_X)),
        pl.BlockSpec((rows, B_GN), lambda b, j: (blk(b, j), P_B)),
        pl.BlockSpec((rows, 128), lambda b, j: (blk(b, j), 0)),
        pl.BlockSpec((1, 128), const),
        pl.BlockSpec((1, 128), const),
        pl.BlockSpec((2 * CHUNK, B_WIDTH), const),
        pl.BlockSpec((rows, A_WIDTH), lambda b, j: (blk(b, j), P_U)),
        pl.BlockSpec((rows, A_WIDTH), lambda b, j: (blk(b, j), P_V)),
        pl.BlockSpec((rows, A_WIDTH), lambda b, j: (blk(b, j), P_ZA)),
        pl.BlockSpec((1, A_WIDTH), const),
        pl.BlockSpec((A_GROUPS, CHUNK, CHUNK), lambda b, j: (0, 0, 0)),
        pl.BlockSpec((A_GROUPS, CHUNK, CHUNK), lambda b, j: (0, 0, 0)),
    ]
    args = [p, p, dt_raw, dt_bias, a_log, e2b, p, p, p, v_gain, w_s, b_s]
    if has_init:
        in_specs.append(pl.BlockSpec((1, B_WIDTH, B_STATE), lambda b, j: (b, 0, 0)))
        args.append(h0)
    m = n_batch * n_chunks * CHUNK
    out_shape = [jax.ShapeDtypeStruct((n_batch * n_chunks, B_GROUPS, B_STATE, B_GROUP_CH), BF16),
                 jax.ShapeDtypeStruct((m, 3 * CHUNK), F32),
                 jax.ShapeDtypeStruct((m, 4 * CHUNK), BF16),
                 jax.ShapeDtypeStruct((m, A_WIDTH), BF16)]
    out_specs = [pl.BlockSpec((cps, B_GROUPS, B_STATE, B_GROUP_CH), lambda b, j: (blk(b, j), 0, 0, 0)),
                 pl.BlockSpec((rows, 3 * CHUNK), lambda b, j: (blk(b, j), 0)),
                 pl.BlockSpec((rows, 4 * CHUNK), lambda b, j: (blk(b, j), 0)),
                 pl.BlockSpec((rows, A_WIDTH), lambda b, j: (blk(b, j), 0))]
    if emit_final:
        out_shape.append(jax.ShapeDtypeStruct((n_batch, B_WIDTH, B_STATE), F32))
        out_specs.append(pl.BlockSpec((1, B_WIDTH, B_STATE), lambda b, j: (b, 0, 0)))
    return pl.pallas_call(
        functools.partial(_ssd_bwd_kernel, n_chunks=n_chunks, has_init=has_init, emit_final=emit_final),
        out_shape=tuple(out_shape),
        grid=(n_batch, steps),
        in_specs=in_specs,
        out_specs=tuple(out_specs),
        scratch_shapes=[pltpu.VMEM((B_GROUPS, B_STATE, B_GROUP_CH), F32)],
        compiler_params=_params(("parallel", "arbitrary")),
        name="l0_ssd_bwd",
    )(*args)


def _ssd_fwd_kernel(*refs, n_chunks, has_init, emit_final):
    it = iter(refs)
    x_ref, bc_ref, sm_ref, smb_ref, z_ref, hin_ref = (next(it), next(it), next(it), next(it),
                                                      next(it), next(it))
    e2f_ref, e2b_ref, dsk_ref, ng_ref = next(it), next(it), next(it), next(it)
    h0_ref = next(it) if has_init else None
    y_ref = next(it)
    hfin_ref = next(it) if emit_final else None
    hf_ref = next(it)

    j = pl.program_id(1)

    @pl.when(j == 0)
    def _():
        _init_state(hf_ref, h0_ref)

    cps = hin_ref.shape[0]
    r = lax.broadcasted_iota(jnp.int32, (CHUNK, CHUNK), 0)
    c = lax.broadcasted_iota(jnp.int32, (CHUNK, CHUNK), 1)
    dsk = dsk_ref[0:1, :] + dsk_ref[1:2, :]
    heads_per_group = B_HEADS // B_GROUPS
    col_head = lax.broadcasted_iota(jnp.int32, (CHUNK, B_GROUP_CH), 1) // B_HEAD_DIM

    for g in range(B_GROUPS):
        cols = slice(g * B_GROUP_CH, (g + 1) * B_GROUP_CH)
        e2f = e2f_ref[:, cols]
        e2b = e2b_ref[:, cols]
        hf_g = hf_ref[g]
        for lc in range(cps):
            rows = slice(lc * CHUNK, (lc + 1) * CHUNK)
            cum2 = sm_ref[rows, 0:CHUNK]
            r2t = sm_ref[rows, CHUNK:2 * CHUNK]
            ldt2t = sm_ref[rows, 2 * CHUNK:3 * CHUNK]
            dec_split = smb_ref[rows, 0:2 * CHUNK]
            dec_f = _expand(dec_split, e2f)
            dec_b = _expand(dec_split, e2b)
            b_g = bc_ref[rows, g * B_STATE:(g + 1) * B_STATE]
            c_g = bc_ref[rows, B_GN + g * B_STATE:B_GN + (g + 1) * B_STATE]
            cb = lax.dot_general(c_g, b_g, (((1,), (1,)), ((), ())), preferred_element_type=F32)
            cb_f = jnp.where(c <= r, cb, 0.0)
            cb_b = jnp.where(c >= r, cb, 0.0)
            y_g = jnp.dot(c_g, hf_g.astype(BF16), preferred_element_type=F32) * dec_f
            y_g = y_g + jnp.dot(c_g, hin_ref[lc, g], preferred_element_type=F32) * dec_b
            x_g = x_ref[rows, cols]
            y_g = y_g + dsk[:, cols] * x_g.astype(F32)
            mats, x_blocks = [], []
            for e in range(heads_per_group):
                h = g * heads_per_group + e
                hb = B_HEADS + h
                arg_f = jnp.minimum(cum2[:, h:h + 1] + r2t[h:h + 1, :], ldt2t[h:h + 1, :])
                arg_b = jnp.minimum(cum2[:, hb:hb + 1] + r2t[hb:hb + 1, :], ldt2t[hb:hb + 1, :])
                mats.append((cb_f * jnp.exp2(arg_f) + cb_b * jnp.exp2(arg_b)).astype(BF16))
                x_blocks.append(jnp.where(col_head == e, x_g, jnp.zeros_like(x_g)))
            y_g = y_g + jnp.dot(jnp.concatenate(mats, axis=1), jnp.concatenate(x_blocks, axis=0),
                                preferred_element_type=F32)
            xw = (x_g.astype(F32) * _expand(smb_ref[rows, 2 * CHUNK:4 * CHUNK], e2f)).astype(BF16)
            upd = lax.dot_general(b_g, xw, (((0,), (0,)), ((), ())), preferred_element_type=F32)
            hf_g = hf_g * dec_f[CHUNK - 1:CHUNK, :] + upd
            yz = y_g * z_ref[rows, cols].astype(F32)
            y_ref[rows, cols] = (_rms_rows(yz) * ng_ref[:, cols]).astype(BF16)
        hf_ref[g] = hf_g

    if emit_final:
        @pl.when(j == pl.num_programs(1) - 1)
        def _():
            _emit_state(hfin_ref, hf_ref)


def _ssd_fwd(p, small, small_bf, hin, e2f, e2b, d_skip, norm_g, h0, n_batch, n_chunks, emit_final, cps=2):
    m = p.shape[0]
    has_init = h0 is not None
    steps = n_chunks // cps
    rows = cps * CHUNK
    blk = lambda b, j: b * steps + j
    const = lambda b, j: (0, 0)
    in_specs = [
        pl.BlockSpec((rows, B_WIDTH), lambda b, j: (blk(b, j), P_X)),
        pl.BlockSpec((rows, 2 * B_GN), lambda b, j: (blk(b, j), P_BC)),
        pl.BlockSpec((rows, 3 * CHUNK), lambda b, j: (blk(b, j), 0)),
        pl.BlockSpec((rows, 4 * CHUNK), lambda b, j: (blk(b, j), 0)),
        pl.BlockSpec((rows, B_WIDTH), lambda b, j: (blk(b, j), P_ZB)),
        pl.BlockSpec((cps, B_GROUPS, B_STATE, B_GROUP_CH), lambda b, j: (blk(b, j), 0, 0, 0)),
        pl.BlockSpec((2 * CHUNK, B_WIDTH), const),
        pl.BlockSpec((2 * CHUNK, B_WIDTH), const),
        pl.BlockSpec((2, B_WIDTH), const),
        pl.BlockSpec((1, B_WIDTH), const),
    ]
    args = [p, p, small, small_bf, p, hin, e2f, e2b, d_skip, norm_g]
    if has_init:
        in_specs.append(pl.BlockSpec((1, B_WIDTH, B_STATE), lambda b, j: (b, 0, 0)))
        args.append(h0)
    out_shape = [jax.ShapeDtypeStruct((m, B_WIDTH), BF16)]
    out_specs = [pl.BlockSpec((rows, B_WIDTH), lambda b, j: (blk(b, j), 0))]
    if emit_final:
        out_shape.append(jax.ShapeDtypeStruct((n_batch, B_WIDTH, B_STATE), F32))
        out_specs.append(pl.BlockSpec((1, B_WIDTH, B_STATE), lambda b, j: (b, 0, 0)))
    return pl.pallas_call(
        functools.partial(_ssd_fwd_kernel, n_chunks=n_chunks, has_init=has_init, emit_final=emit_final),
        out_shape=tuple(out_shape),
        grid=(n_batch, steps),
        in_specs=in_specs,
        out_specs=tuple(out_specs),
        scratch_shapes=[pltpu.VMEM((B_GROUPS, B_STATE, B_GROUP_CH), F32)],
        compiler_params=_params(("parallel", "arbitrary")),
        name="l0_ssd_fwd",
    )(*args)


def _outproj_kernel(*refs, n_in):
    acts, ws = refs[:n_in], refs[n_in:2 * n_in]
    x_ref, mod_ref, g_ref, o_ref = refs[2 * n_in:]
    acc = jnp.dot(acts[0][...], ws[0][...], preferred_element_type=F32)
    for a_ref, w_ref in zip(acts[1:], ws[1:]):
        acc = acc + jnp.dot(a_ref[...], w_ref[...], preferred_element_type=F32)
    gain = mod_ref[0, 2:3, :] * g_ref[...]
    r = lax.rsqrt(jnp.mean(acc * acc, axis=-1, keepdims=True) + EPS)
    o_ref[...] = x_ref[...] + acc * r * gain


def _outproj(acts, w, x, mod, g, rows_per_cond, tm=512):
    m = x.shape[0]
    n_in = len(acts)
    kdim = acts[0].shape[1]
    row = _mod_row_map(rows_per_cond, tm)
    in_specs = [pl.BlockSpec((tm, kdim), lambda i: (i, 0)) for _ in acts]
    in_specs += [pl.BlockSpec((kdim, D_MODEL), functools.partial(lambda i, k: (k, 0), k=k),
                              pipeline_mode=pl.Buffered(1)) for k in range(n_in)]
    in_specs += [pl.BlockSpec((tm, D_MODEL), lambda i: (i, 0)),
                 pl.BlockSpec((1, 3, D_MODEL), lambda i: (row(i), 0, 0)),
                 pl.BlockSpec((1, D_MODEL), lambda i: (0, 0))]
    return pl.pallas_call(
        functools.partial(_outproj_kernel, n_in=n_in),
        out_shape=jax.ShapeDtypeStruct((m, D_MODEL), F32),
        grid=(m // tm,),
        in_specs=in_specs,
        out_specs=pl.BlockSpec((tm, D_MODEL), lambda i: (i, 0)),
        compiler_params=_params(("parallel",)),
        name="outproj",
    )(*acts, *([w] * n_in), x, mod, g)


Q_SCALE = C_HEAD_DIM ** -0.5 * LOG2E


def _inproj1_kernel(*refs, tn, emit_kv):
    it = iter(refs)
    x_ref, mod_ref, g_ref, w_ref, qn_ref, kn_ref = next(it), next(it), next(it), next(it), next(it), next(it)
    p_ref = next(it)
    k_ref = next(it) if emit_kv else None
    v_ref = next(it) if emit_kv else None
    h_ref = next(it)

    j = pl.program_id(1)
    nq = C_WIDTH // tn

    @pl.when(j == 0)
    def _():
        h_ref[...] = _prenorm(x_ref[...], mod_ref, g_ref)

    def sub_dot(s):
        return jnp.dot(h_ref[...], w_ref[:, s * SUB:(s + 1) * SUB], preferred_element_type=F32)

    @pl.when(j < nq)
    def _():
        for s in range(tn // SUB):
            acc = sub_dot(s)
            for hh in range(SUB // C_HEAD_DIM):
                q = _rms_rows(acc[:, hh * C_HEAD_DIM:(hh + 1) * C_HEAD_DIM]) * (qn_ref[...] * Q_SCALE)
                col = s * SUB + hh * C_HEAD_DIM
                p_ref[:, col:col + C_HEAD_DIM] = q.astype(BF16)

    @pl.when(j == nq)
    def _():
        for s in range(tn // SUB):
            acc = sub_dot(s)
            if s * SUB < C_KV_WIDTH:
                for hh in range(SUB // C_HEAD_DIM):
                    k = _rms_rows(acc[:, hh * C_HEAD_DIM:(hh + 1) * C_HEAD_DIM]) * kn_ref[...]
                    col = s * SUB + hh * C_HEAD_DIM
                    p_ref[:, col:col + C_HEAD_DIM] = k.astype(BF16)
                    if emit_kv:
                        k_ref[:, col:col + C_HEAD_DIM] = k
            else:
                p_ref[:, s * SUB:(s + 1) * SUB] = acc.astype(BF16)
                if emit_kv:
                    v_ref[:, s * SUB - C_KV_WIDTH:(s + 1) * SUB - C_KV_WIDTH] = acc

    @pl.when(j > nq)
    def _():
        for s in range(tn // SUB):
            p_ref[:, s * SUB:(s + 1) * SUB] = _silu_tanh(sub_dot(s)).astype(BF16)


def _inproj1(x, mod, g, w, q_norm, k_norm, rows_per_cond, emit_kv, tm=1024, tn=1024):
    m = x.shape[0]
    assert tn == 2 * C_KV_WIDTH
    row = _mod_row_map(rows_per_cond, tm)
    nq = C_WIDTH // tn
    nblk = L1_IN // tn
    out_col = lambda j: jnp.where(j < nq, j, jnp.where(j == nq, nblk - 1, j - 1))
    in_specs = [pl.BlockSpec((tm, D_MODEL), lambda i, j: (i, 0)),
                pl.BlockSpec((1, 3, D_MODEL), lambda i, j: (row(i), 0, 0)),
                pl.BlockSpec((1, D_MODEL), lambda i, j: (0, 0)),
                pl.BlockSpec((D_MODEL, tn), lambda i, j: (0, j)),
                pl.BlockSpec((1, C_HEAD_DIM), lambda i, j: (0, 0)),
                pl.BlockSpec((1, C_HEAD_DIM), lambda i, j: (0, 0))]
    out_shape = [jax.ShapeDtypeStruct((m, L1_IN), BF16)]
    out_specs = [pl.BlockSpec((tm, tn), lambda i, j: (i, out_col(j)))]
    if emit_kv:
        out_shape += [jax.ShapeDtypeStruct((m, C_KV_WIDTH), F32)] * 2
        out_specs += [pl.BlockSpec((tm, C_KV_WIDTH), lambda i, j: (i, 0))] * 2
    return pl.pallas_call(
        functools.partial(_inproj1_kernel, tn=tn, emit_kv=emit_kv),
        out_shape=tuple(out_shape),
        grid=(m // tm, nblk),
        in_specs=in_specs,
        out_specs=tuple(out_specs),
        scratch_shapes=[pltpu.VMEM((tm, D_MODEL), BF16)],
        compiler_params=_params(("parallel", "arbitrary")),
        name="l1_inproj",
    )(x, mod, g, w, q_norm, k_norm)


def _rope(x_bf16, cos, sin_signed):
    x = x_bf16.astype(F32)
    lane = lax.broadcasted_iota(jnp.int32, x.shape, 1)
    swapped = jnp.where((lane // 32) % 2 == 0, pltpu.roll(x, C_HEAD_DIM - 32, axis=1), pltpu.roll(x, 32, axis=1))
    return (x * cos + swapped * sin_signed).astype(BF16)


def _attn_kernel(*refs, has_ctx, use_rope):
    it = iter(refs)
    q_ref, z_ref, kn_ref, vn_ref = next(it), next(it), next(it), next(it)
    kc_ref = next(it) if has_ctx else None
    vc_ref = next(it) if has_ctx else None
    if use_rope:
        cosq_ref, sinq_ref, cosk_ref, sink_ref = next(it), next(it), next(it), next(it)
    o_ref, kk_ref, vv_ref = next(it), next(it), next(it)
    past = kc_ref.shape[1] if has_ctx else 0
    n_new = kn_ref.shape[0]
    rep = C_HEADS // C_KV_HEADS
    nt = (((1,), (1,)), ((), ()))

    @pl.when(pl.program_id(1) == 0)
    def _():
        for kv in range(C_KV_HEADS):
            kcols = slice(kv * C_HEAD_DIM, (kv + 1) * C_HEAD_DIM)
            if has_ctx:
                kk_ref[kv, 0:past, :] = kc_ref[0, :, kcols]
                vv_ref[kv, 0:past, 0:C_HEAD_DIM] = vc_ref[0, :, kcols]
            k_new = kn_ref[:, kcols]
            kk_ref[kv, past:past + n_new, :] = _rope(k_new, cosk_ref[...], sink_ref[...]) if use_rope else k_new
            vv_ref[kv, past:past + n_new, 0:C_HEAD_DIM] = vn_ref[:, kcols]
            vv_ref[kv, :, C_HEAD_DIM:2 * C_HEAD_DIM] = jnp.ones((past + n_new, C_HEAD_DIM), BF16)

    for kv in range(C_KV_HEADS):
        for e in range(rep):
            cols = slice((kv * rep + e) * C_HEAD_DIM, (kv * rep + e + 1) * C_HEAD_DIM)
            q = _rope(q_ref[:, cols], cosq_ref[...], sinq_ref[...]) if use_rope else q_ref[:, cols]
            s = lax.dot_general(q, kk_ref[kv], nt, preferred_element_type=F32)
            mx = jnp.max(s, axis=-1, keepdims=True)
            o = jnp.dot(jnp.exp2(s - mx).astype(BF16), vv_ref[kv], preferred_element_type=F32)
            out = o[:, :C_HEAD_DIM] * (1.0 / o[:, C_HEAD_DIM:]) * z_ref[:, cols].astype(F32)
            o_ref[:, cols] = out.astype(BF16)


def _attention(qkvz, seq_len, n_batch, tq, ctx_k=None, ctx_v=None, rope=None):
    m = qkvz.shape[0]
    qb = seq_len // tq
    kcol = (2 * C_WIDTH) // C_KV_WIDTH
    has_ctx = ctx_k is not None
    use_rope = rope is not None
    past = ctx_k.shape[1] if has_ctx else 0
    in_specs = [pl.BlockSpec((tq, C_WIDTH), lambda b, i: (b * qb + i, 0)),
                pl.BlockSpec((tq, C_WIDTH), lambda b, i: (b * qb + i, 1)),
                pl.BlockSpec((seq_len, C_KV_WIDTH), lambda b, i: (b, kcol)),
                pl.BlockSpec((seq_len, C_KV_WIDTH), lambda b, i: (b, kcol + 1))]
    args = [qkvz, qkvz, qkvz, qkvz]
    if has_ctx:
        in_specs += [pl.BlockSpec((1, past, C_KV_WIDTH), lambda b, i: (b, 0, 0))] * 2
        args += [ctx_k, ctx_v]
    if use_rope:
        in_specs += [pl.BlockSpec((tq, C_HEAD_DIM), lambda b, i: (i, 0))] * 2
        in_specs += [pl.BlockSpec((seq_len, C_HEAD_DIM), lambda b, i: (0, 0))] * 2
        args += [*rope, *rope]
    return pl.pallas_call(
        functools.partial(_attn_kernel, has_ctx=has_ctx, use_rope=use_rope),
        out_shape=jax.ShapeDtypeStruct((m, C_WIDTH), BF16),
        grid=(n_batch, qb),
        in_specs=in_specs,
        out_specs=pl.BlockSpec((tq, C_WIDTH), lambda b, i: (b * qb + i, 0)),
        scratch_shapes=[pltpu.VMEM((C_KV_HEADS, past + seq_len, C_HEAD_DIM), BF16),
                        pltpu.VMEM((C_KV_HEADS, past + seq_len, 2 * C_HEAD_DIM), BF16)],
        compiler_params=_params(("parallel", "arbitrary")),
        name="l1_attention",
    )(*args)


def _expand_matrix(first_row):
    e = np.zeros((2 * CHUNK, B_WIDTH), np.float32)
    ch = np.arange(B_WIDTH)
    e[first_row + ch // B_HEAD_DIM, ch] = 1.0
    e[CHUNK + first_row + ch // B_HEAD_DIM, ch] = 1.0
    return jnp.asarray(e, BF16)


def _rope_tables(n):
    t = np.arange(n)
    pos = np.stack([t // GRID_W, t % GRID_W], axis=1).astype(np.float64)
    half = C_HEAD_DIM // 2
    inv = ROPE_THETA ** (-np.arange(0, half, 2, dtype=np.float64) / half)
    ang = pos[:, :, None] * inv[None, None, :]
    cos = np.concatenate([np.cos(ang), np.cos(ang)], axis=-1).reshape(n, C_HEAD_DIM)
    sin = np.concatenate([-np.sin(ang), np.sin(ang)], axis=-1).reshape(n, C_HEAD_DIM)
    return jnp.asarray(cos, F32), jnp.asarray(sin, F32)


def _pad_lanes(x, width=128):
    return jnp.pad(x, ((0, 0), (0, width - x.shape[1])))


def kernel(x_prompt, x_sample, state_l0_ssm_fwd, state_l0_ssm_bwd, cache_l1_k, cache_l1_v, c, c_ctx, mod_w0, mod_b0, norm_pre0, norm_post0, l0_w_in, l0_v_gain, l0_w_s, l0_b_s, l0_conv_w, l0_conv_b, l0_dt_bias, l0_a_log, l0_d_skip, l0_ssm_norm, l0_w_out, mod_w1, mod_b1, norm_pre1, norm_post1, l1_w_in, l1_q_norm, l1_k_norm, l1_w_out):
    pb, pn, d = x_prompt.shape
    sb, sn, _ = x_sample.shape
    xp = x_prompt.reshape(pb * pn, d)
    xs = x_sample.reshape(sb * sn, d)
    row = lambda v: v.reshape(1, -1)

    cond = jnp.zeros((MOD_ROWS, d), F32).at[:sb].set(c).at[CTX_ROW].set(c_ctx)
    w0_main = l0_w_in.astype(BF16)
    w0_dt = _pad_lanes(l0_w_in[:, L0_MAIN:]).astype(BF16)
    w0_out = l0_w_out.astype(BF16)
    ws = l0_w_s.astype(BF16)
    bs = jnp.broadcast_to(l0_b_s[:, :, None], (A_GROUPS, CHUNK, CHUNK))
    conv_w = jnp.pad(l0_conv_w, ((0, 8 - B_CONV), (0, 0)))
    dt_bias = _pad_lanes(l0_dt_bias.reshape(1, 2 * B_HEADS))
    a_log = _pad_lanes(l0_a_log.reshape(1, 2 * B_HEADS))
    d_skip = jnp.repeat(l0_d_skip, B_HEAD_DIM, axis=1)
    e2f, e2b = _expand_matrix(0), _expand_matrix(B_HEADS)
    w1_in = l1_w_in.astype(BF16)
    w1_out = l1_w_out.astype(BF16)
    ctx_k = cache_l1_k.reshape(sb, -1, C_KV_WIDTH).astype(BF16)
    ctx_v = cache_l1_v.reshape(sb, -1, C_KV_WIDTH).astype(BF16)
    rope = _rope_tables(sn)

    mod0 = _modulation(cond, mod_w0, mod_b0)
    mod1 = _modulation(cond, mod_w1, mod_b1)

    def layer0(x, rows_per_cond, n_batch, seq, h0f, h0b, emit_final):
        n_chunks = seq // CHUNK
        p, dt_raw = _inproj0(x, mod0, row(norm_pre0), w0_main, w0_dt, conv_w, row(l0_conv_b),
                             rows_per_cond, seq)
        bwd = _ssd_bwd(p, dt_raw, dt_bias, a_log, e2b, row(l0_v_gain), ws, bs, h0b,
                       n_batch, n_chunks, emit_final, cps=min(4, n_chunks))
        fwd = _ssd_fwd(p, bwd[1], bwd[2], bwd[0], e2f, e2b, d_skip, row(l0_ssm_norm),
                       h0f, n_batch, n_chunks, emit_final, cps=min(4, n_chunks))
        x1 = _outproj([bwd[3], fwd[0]], w0_out, x, mod0, row(norm_post0), rows_per_cond)
        return x1, (fwd[1] if emit_final else None), (bwd[4] if emit_final else None)

    st_shape = (sb, B_WIDTH, B_STATE)
    xp1, new_f, new_b = layer0(xp, None, pb, pn, None, None, True)
    xs1, _, _ = layer0(xs, sn, sb, sn, state_l0_ssm_fwd.reshape(st_shape),
                       state_l0_ssm_bwd.reshape(st_shape), False)

    qp, new_k, new_v = _inproj1(xp1, mod1, row(norm_pre1), w1_in, row(l1_q_norm), row(l1_k_norm),
                                None, True)
    op = _attention(qp, pn, pb, pn)
    yp = _outproj([op], w1_out, xp1, mod1, row(norm_post1), None)

    (qs,) = _inproj1(xs1, mod1, row(norm_pre1), w1_in, row(l1_q_norm), row(l1_k_norm), sn, False)
    os_ = _attention(qs, sn, sb, min(sn, 512), ctx_k, ctx_v, rope)
    ys = _outproj([os_], w1_out, xs1, mod1, row(norm_post1), sn)

    return (yp.reshape(pb, pn, d), ys.reshape(sb, sn, d),
            new_f.reshape(pb, B_HEADS, B_HEAD_DIM, B_STATE),
            new_b.reshape(pb, B_HEADS, B_HEAD_DIM, B_STATE),
            new_k.reshape(pb, pn, C_KV_HEADS, C_HEAD_DIM),
            new_v.reshape(pb, pn, C_KV_HEADS, C_HEAD_DIM))
```
